```python
import jax, jax.numpy as jnp
from jax import lax
import numpy as np

D_MODEL = 2048
BATCH = 8
SEQ = 4096
DEPTH = 4

N_MIXERS = 2
CONV_WIDTH = 4
RMS_EPS = 1e-6
N_MODS = 6

D_RNN = D_MODEL
LRU_HEADS = 16
LRU_HEAD_DIM = D_RNN // LRU_HEADS
LRU_C = 8.0
LRU_MIN_RAD = 0.9
LRU_MAX_RAD = 0.999

GDN_HEAD_DIM = 128
GDN_KEY_HEADS = D_MODEL // GDN_HEAD_DIM
GDN_VALUE_HEADS = 2 * GDN_KEY_HEADS
GDN_GROUP = GDN_VALUE_HEADS // GDN_KEY_HEADS
GDN_QK_DIM = GDN_KEY_HEADS * GDN_HEAD_DIM
GDN_V_DIM = GDN_VALUE_HEADS * GDN_HEAD_DIM
GDN_CONV_DIM = 2 * GDN_QK_DIM + GDN_V_DIM
GDN_PROJ = GDN_CONV_DIM + GDN_V_DIM + 2 * GDN_VALUE_HEADS
CHUNK = 64
L2_EPS = 1e-6

N_EXPERTS = 32
TOP_K = 4
D_FF_EXPERT = (3 * D_MODEL) // 8
SWIGLU_ALPHA = 1.702
SWIGLU_LIMIT = 7.0
MOE_BLOCK = 256

N_LRU_LAYERS = (DEPTH + 1) // 2
N_GDN_LAYERS = DEPTH // 2

kernel_name = "hybrid_rglru_gdn_moe_adaln"


def rms_norm(x, w):
    xf = x.astype(jnp.float32)
    y = xf * lax.rsqrt(jnp.mean(xf * xf, axis=-1, keepdims=True) + RMS_EPS)
    return (y * w.astype(jnp.float32)).astype(x.dtype)


def l2_norm(x):
    xf = x.astype(jnp.float32)
    return xf * lax.rsqrt(jnp.sum(xf * xf, axis=-1, keepdims=True) + L2_EPS)


def causal_depthwise_conv(x, w):
    W, C = w.shape
    return lax.conv_general_dilated(
        x, w[:, None, :], window_strides=(1,), padding=[(W - 1, 0)],
        dimension_numbers=("NWC", "WIO", "NWC"), feature_group_count=C)


def linear_recurrence(a, b):
    def combine(l, r):
        return l[0] * r[0], r[0] * l[1] + r[1]
    _, h = lax.associative_scan(combine, (a, b), axis=1)
    return h


def rglru_mixer(u, w_in, conv_w, conv_b, w_rg, b_rg, w_ig, b_ig, lam, w_out):
    B, S, _ = u.shape
    proj = u @ w_in
    xb, yb = proj[..., :D_RNN], proj[..., D_RNN:]
    yb = jax.nn.gelu(yb)
    xb = causal_depthwise_conv(xb, conv_w) + conv_b
    xh = xb.reshape(B, S, LRU_HEADS, LRU_HEAD_DIM)
    r = jax.nn.sigmoid(jnp.einsum("bshi,hij->bshj", xh, w_rg).reshape(B, S, D_RNN) + b_rg)
    i = jax.nn.sigmoid(jnp.einsum("bshi,hij->bshj", xh, w_ig).reshape(B, S, D_RNN) + b_ig)
    r = r.astype(jnp.float32)
    log_a = -LRU_C * r * jax.nn.softplus(-lam.astype(jnp.float32))
    a = jnp.exp(log_a)
    gated_x = (i * xb).astype(jnp.float32)
    b_t = jnp.sqrt(jnp.maximum(1.0 - jnp.exp(2.0 * log_a), 0.0)) * gated_x
    h = linear_recurrence(a, b_t).astype(u.dtype)
    return (h * yb) @ w_out


def chunk_gated_delta_rule(q, k, v, g, beta):
    B, S, H, DK = q.shape
    DV = v.shape[-1]
    N = S // CHUNK
    out_dtype = v.dtype

    def to_chunks(t):
        return t.astype(jnp.float32).reshape(B, N, CHUNK, H, -1).transpose(0, 3, 1, 2, 4)

    q, k, v = to_chunks(q), to_chunks(k), to_chunks(v)
    g = g.astype(jnp.float32).reshape(B, N, CHUNK, H).transpose(0, 3, 1, 2)
    beta = beta.astype(jnp.float32).reshape(B, N, CHUNK, H).transpose(0, 3, 1, 2)
    g = jnp.cumsum(g, axis=-1)

    causal = jnp.tril(jnp.ones((CHUNK, CHUNK), dtype=bool))
    strict = jnp.tril(jnp.ones((CHUNK, CHUNK), dtype=bool), k=-1)
    decay = jnp.exp(jnp.where(causal, g[..., :, None] - g[..., None, :], -jnp.inf))

    k_beta = k * beta[..., None]
    v_beta = v * beta[..., None]
    a_mat = jnp.where(strict, jnp.einsum("bhnid,bhnjd->bhnij", k_beta, k) * decay, 0.0)
    t_mat = a_mat + jnp.eye(CHUNK, dtype=jnp.float32)
    rhs = jnp.concatenate([v_beta, k_beta * jnp.exp(g)[..., None]], axis=-1)
    sol = lax.linalg.triangular_solve(t_mat, rhs, left_side=True, lower=True,
                                      unit_diagonal=True)
    u_val, w_dec = sol[..., :DV], sol[..., DV:]

    qk = jnp.einsum("bhnid,bhnjd->bhnij", q, k) * decay
    q_dec = q * jnp.exp(g)[..., None]
    k_dec = k * jnp.exp(g[..., -1:] - g)[..., None]
    g_last = jnp.exp(g[..., -1])

    def step(state, inp):
        qk_i, q_i, k_i, u_i, w_i, gl_i = inp
        v_new = u_i - jnp.einsum("bhck,bhkv->bhcv", w_i, state)
        o_i = (jnp.einsum("bhck,bhkv->bhcv", q_i, state)
               + jnp.einsum("bhij,bhjv->bhiv", qk_i, v_new))
        state = state * gl_i[..., None, None] + jnp.einsum("bhck,bhcv->bhkv", k_i, v_new)
        return state, o_i

    xs = (jnp.moveaxis(qk, 2, 0), jnp.moveaxis(q_dec, 2, 0), jnp.moveaxis(k_dec, 2, 0),
          jnp.moveaxis(u_val, 2, 0), jnp.moveaxis(w_dec, 2, 0), jnp.moveaxis(g_last, 2, 0))
    state0 = jnp.zeros((B, H, DK, DV), jnp.float32)
    _, o = lax.scan(step, state0, xs)
    return o.transpose(1, 0, 3, 2, 4).reshape(B, S, H, DV).astype(out_dtype)


def gdn_mixer(u, w_in, conv_w, a_log, dt_bias, norm_w, w_out):
    B, S, _ = u.shape
    proj = u @ w_in
    qkv = jax.nn.silu(causal_depthwise_conv(proj[..., :GDN_CONV_DIM], conv_w))
    z = proj[..., GDN_CONV_DIM:GDN_CONV_DIM + GDN_V_DIM]
    b_logit = proj[..., GDN_CONV_DIM + GDN_V_DIM:GDN_CONV_DIM + GDN_V_DIM + GDN_VALUE_HEADS]
    a_logit = proj[..., GDN_CONV_DIM + GDN_V_DIM + GDN_VALUE_HEADS:]
    q = qkv[..., :GDN_QK_DIM].reshape(B, S, GDN_KEY_HEADS, GDN_HEAD_DIM)
    k = qkv[..., GDN_QK_DIM:2 * GDN_QK_DIM].reshape(B, S, GDN_KEY_HEADS, GDN_HEAD_DIM)
    v = qkv[..., 2 * GDN_QK_DIM:].reshape(B, S, GDN_VALUE_HEADS, GDN_HEAD_DIM)
    q = l2_norm(q) * (GDN_HEAD_DIM ** -0.5)
    k = l2_norm(k)
    q = jnp.repeat(q, GDN_GROUP, axis=2)
    k = jnp.repeat(k, GDN_GROUP, axis=2)
    beta = jax.nn.sigmoid(b_logit.astype(jnp.float32))
    g = -jnp.exp(a_log.astype(jnp.float32)) * jax.nn.softplus(
        a_logit.astype(jnp.float32) + dt_bias.astype(jnp.float32))
    o = chunk_gated_delta_rule(q, k, v, g, beta)
    o = rms_norm(o, norm_w) * jax.nn.silu(z.reshape(B, S, GDN_VALUE_HEADS, GDN_HEAD_DIM))
    return o.reshape(B, S, GDN_V_DIM) @ w_out


def clamped_swiglu(gu):
    glu, lin = gu[..., :D_FF_EXPERT], gu[..., D_FF_EXPERT:]
    glu = jnp.minimum(glu, SWIGLU_LIMIT)
    lin = jnp.clip(lin, -SWIGLU_LIMIT, SWIGLU_LIMIT)
    return glu * jax.nn.sigmoid(SWIGLU_ALPHA * glu) * (lin + 1.0)


def moe_ffn(u, w_router, b_router, w_gu, b_gu, w_down, b_down):
    B, S, D = u.shape
    T = B * S
    xt = u.reshape(T, D)
    logits = (xt @ w_router + b_router).astype(jnp.float32)
    top_vals, top_idx = lax.top_k(logits, TOP_K)
    top_w = jax.nn.softmax(top_vals, axis=-1)
    A = T * TOP_K
    e_flat = top_idx.reshape(A)
    w_flat = top_w.reshape(A)
    tok_flat = jnp.arange(A, dtype=jnp.int32) // TOP_K
    order = jnp.argsort(e_flat)
    e_s, tok_s, w_s = e_flat[order], tok_flat[order], w_flat[order]
    counts = jnp.bincount(e_flat, length=N_EXPERTS)
    padded = (counts + MOE_BLOCK - 1) // MOE_BLOCK * MOE_BLOCK
    grp_start = jnp.cumsum(counts) - counts
    pad_end = jnp.cumsum(padded)
    pad_start = pad_end - padded
    dest = pad_start[e_s] + (jnp.arange(A, dtype=jnp.int32) - grp_start[e_s])
    n_blocks = -(-(A + N_EXPERTS * (MOE_BLOCK - 1)) // MOE_BLOCK)
    R = n_blocks * MOE_BLOCK
    row_tok = jnp.full((R,), T, dtype=jnp.int32).at[dest].set(tok_s)
    row_w = jnp.zeros((R,), dtype=u.dtype).at[dest].set(w_s.astype(u.dtype))
    blk_start = jnp.arange(n_blocks, dtype=jnp.int32) * MOE_BLOCK
    blk_expert = jnp.minimum(jnp.searchsorted(pad_end, blk_start, side="right"),
                             N_EXPERTS - 1)
    x_pad = jnp.concatenate([xt, jnp.zeros((1, D), xt.dtype)], axis=0)

    def body(out, inp):
        e, toks, wts = inp
        xb = x_pad[toks]
        hid = clamped_swiglu(xb @ w_gu[e] + b_gu[e])
        y = hid @ w_down[e] + b_down[e]
        return out.at[toks].add(y * wts[:, None]), None

    out0 = jnp.zeros((T + 1, D), u.dtype)
    out, _ = lax.scan(body, out0, (blk_expert, row_tok.reshape(n_blocks, MOE_BLOCK),
                                   row_w.reshape(n_blocks, MOE_BLOCK)))
    return out[:T].reshape(B, S, D)


def setup_inputs(seed: int = 0) -> dict:
    key = jax.random.key(seed)
    ks = jax.random.split(key, 32)
    D = D_MODEL
    f32 = jnp.float32

    def nrm(k, shape, scale):
        return jax.random.normal(k, shape, f32) * scale

    u_lam = jax.random.uniform(ks[14], (N_LRU_LAYERS, D_RNN), f32, LRU_MIN_RAD, LRU_MAX_RAD)
    return {
        "x": nrm(ks[0], (BATCH, SEQ, D), 1.0),
        "c": nrm(ks[1], (BATCH, D), 1.0),
        "w_ada": nrm(ks[2], (D, N_MODS * D), 0.5 * D ** -0.5),
        "ada_table": nrm(ks[3], (DEPTH, N_MODS, D), 0.1),
        "norm_mix": 1.0 + nrm(ks[4], (DEPTH, D), 0.02),
        "norm_ffn": 1.0 + nrm(ks[5], (DEPTH, D), 0.02),
        "norm_final": 1.0 + nrm(ks[6], (D,), 0.02),
        "lru_w_in": nrm(ks[7], (N_LRU_LAYERS, D, 2 * D_RNN), D ** -0.5),
        "lru_conv_w": nrm(ks[8], (N_LRU_LAYERS, CONV_WIDTH, D_RNN), CONV_WIDTH ** -0.5),
        "lru_conv_b": nrm(ks[9], (N_LRU_LAYERS, D_RNN), 0.01),
        "lru_w_rg": nrm(ks[10], (N_LRU_LAYERS, LRU_HEADS, LRU_HEAD_DIM, LRU_HEAD_DIM), LRU_HEAD_DIM ** -0.5),
        "lru_b_rg": nrm(ks[11], (N_LRU_LAYERS, D_RNN), 0.01),
        "lru_w_ig": nrm(ks[12], (N_LRU_LAYERS, LRU_HEADS, LRU_HEAD_DIM, LRU_HEAD_DIM), LRU_HEAD_DIM ** -0.5),
        "lru_b_ig": nrm(ks[13], (N_LRU_LAYERS, D_RNN), 0.01),
        "lru_lambda": jnp.log(u_lam) - jnp.log1p(-u_lam),
        "lru_w_out": nrm(ks[15], (N_LRU_LAYERS, D_RNN, D), D_RNN ** -0.5),
        "gdn_w_in": nrm(ks[16], (N_GDN_LAYERS, D, GDN_PROJ), D ** -0.5),
        "gdn_conv_w": nrm(ks[17], (N_GDN_LAYERS, CONV_WIDTH, GDN_CONV_DIM), CONV_WIDTH ** -0.5),
        "gdn_a_log": jnp.log(jax.random.uniform(ks[18], (N_GDN_LAYERS, GDN_VALUE_HEADS), f32, 1.0, 16.0)),
        "gdn_dt_bias": nrm(ks[19], (N_GDN_LAYERS, GDN_VALUE_HEADS), 0.1),
        "gdn_norm": 1.0 + nrm(ks[20], (N_GDN_LAYERS, GDN_HEAD_DIM), 0.02),
        "gdn_w_out": nrm(ks[21], (N_GDN_LAYERS, GDN_V_DIM, D), GDN_V_DIM ** -0.5),
        "moe_w_router": nrm(ks[22], (DEPTH, D, N_EXPERTS), D ** -0.5),
        "moe_b_router": nrm(ks[23], (DEPTH, N_EXPERTS), 0.01),
        "moe_w_gate_up": nrm(ks[24], (DEPTH, N_EXPERTS, D, 2 * D_FF_EXPERT), D ** -0.5),
        "moe_b_gate_up": nrm(ks[25], (DEPTH, N_EXPERTS, 2 * D_FF_EXPERT), 0.01),
        "moe_w_down": nrm(ks[26], (DEPTH, N_EXPERTS, D_FF_EXPERT, D), D_FF_EXPERT ** -0.5),
        "moe_b_down": nrm(ks[27], (DEPTH, N_EXPERTS, D), 0.01),
    }


def reference(x, c, w_ada, ada_table, norm_mix, norm_ffn, norm_final,
              lru_w_in, lru_conv_w, lru_conv_b, lru_w_rg, lru_b_rg, lru_w_ig, lru_b_ig,
              lru_lambda, lru_w_out,
              gdn_w_in, gdn_conv_w, gdn_a_log, gdn_dt_bias, gdn_norm, gdn_w_out,
              moe_w_router, moe_b_router, moe_w_gate_up, moe_b_gate_up, moe_w_down, moe_b_down):
    B, _, D = x.shape
    cond = (jax.nn.silu(c) @ w_ada).reshape(B, N_MODS, D)
    h = x
    for layer in range(DEPTH):
        mods = cond + ada_table[layer]
        shift_m, scale_m, gate_m, shift_f, scale_f, gate_f = [mods[:, j, None, :] for j in range(N_MODS)]
        u = rms_norm(h, norm_mix[layer]) * (1.0 + scale_m) + shift_m
        j = layer // N_MIXERS
        if layer % N_MIXERS == 0:
            mix = rglru_mixer(u, lru_w_in[j], lru_conv_w[j], lru_conv_b[j], lru_w_rg[j], lru_b_rg[j],
                              lru_w_ig[j], lru_b_ig[j], lru_lambda[j], lru_w_out[j])
        else:
            mix = gdn_mixer(u, gdn_w_in[j], gdn_conv_w[j], gdn_a_log[j], gdn_dt_bias[j],
                            gdn_norm[j], gdn_w_out[j])
        h = h + gate_m * mix
        u = rms_norm(h, norm_ffn[layer]) * (1.0 + scale_f) + shift_f
        h = h + gate_f * moe_ffn(u, moe_w_router[layer], moe_b_router[layer], moe_w_gate_up[layer],
                                 moe_b_gate_up[layer], moe_w_down[layer], moe_b_down[layer])
    return rms_norm(h, norm_final)
```

```python
import functools

import jax
import jax.numpy as jnp
from jax import lax
from jax.experimental import pallas as pl
from jax.experimental.pallas import tpu as pltpu

f32 = jnp.float32
bf16 = jnp.bfloat16
i32 = jnp.int32

N_MODS = 6
CONV_WIDTH = 4
RMS_EPS = 1e-6
L2_EPS = 1e-6

LRU_HEADS = 16
LRU_HEAD_DIM = 128
LRU_C = 8.0

GDN_HEAD_DIM = 128
GDN_KEY_HEADS = 16
GDN_VALUE_HEADS = 32
GDN_QK_DIM = GDN_KEY_HEADS * GDN_HEAD_DIM
GDN_V_DIM = GDN_VALUE_HEADS * GDN_HEAD_DIM
GDN_CONV_DIM = 2 * GDN_QK_DIM + GDN_V_DIM
CHUNK = 64

N_EXPERTS = 32
TOP_K = 4
SWIGLU_ALPHA = 1.702
SWIGLU_LIMIT = 7.0
MOE_ROWS = 256

VMEM_LIMIT_BYTES = 56 * 1024 * 1024
SUBLANES = 8


def _params(*sem):
    return pltpu.CompilerParams(dimension_semantics=sem, vmem_limit_bytes=VMEM_LIMIT_BYTES)


def _dot(a, b):
    return jnp.dot(a.astype(bf16), b.astype(bf16), preferred_element_type=f32)


def _dot_nt(a, b):
    return lax.dot_general(a.astype(bf16), b.astype(bf16), (((1,), (1,)), ((), ())),
                           preferred_element_type=f32)


def _dot_tn(a, b):
    return lax.dot_general(a.astype(bf16), b.astype(bf16), (((0,), (0,)), ((), ())),
                           preferred_element_type=f32)


def _silu(x):
    return x * jax.nn.sigmoid(x)


def _softplus(x):
    return jnp.maximum(x, 0.0) + jnp.log(1.0 + jnp.exp(-jnp.abs(x)))


def _modulated_norm(x, nw, scale, shift):
    ms = jnp.mean(x * x, axis=-1, keepdims=True)
    y = x * lax.rsqrt(ms + RMS_EPS) * nw
    return y * (1.0 + scale) + shift


def _ada_body(c_ref, w_ref, o_ref):
    o_ref[...] = _dot(_silu(c_ref[...]), w_ref[...])


def _ada(c, w_ada):
    B, D = c.shape
    N = w_ada.shape[1]
    tn = 1024
    return pl.pallas_call(
        _ada_body,
        grid=(N // tn,),
        in_specs=[pl.BlockSpec((B, D), lambda j: (0, 0)),
                  pl.BlockSpec((D, tn), lambda j: (0, j))],
        out_specs=pl.BlockSpec((B, tn), lambda j: (0, j)),
        out_shape=jax.ShapeDtypeStruct((B, N), f32),
        compiler_params=_params("arbitrary"),
        name="ada",
    )(c, w_ada)


def _norm_mm_body(h_ref, nw_ref, sc_ref, sh_ref, w_ref, *rest, has_small):
    if has_small:
        ws_ref, o_ref, os_ref, u_ref = rest
    else:
        o_ref, u_ref = rest

    @pl.when(pl.program_id(1) == 0)
    def _():
        u = _modulated_norm(h_ref[...], nw_ref[...], sc_ref[0], sh_ref[0]).astype(bf16)
        u_ref[...] = u
        if has_small:
            os_ref[...] = jnp.dot(u, ws_ref[...], preferred_element_type=f32)

    o_ref[...] = jnp.dot(u_ref[...], w_ref[...], preferred_element_type=f32).astype(o_ref.dtype)


def _norm_matmul(h, nw, scale, shift, w, w_small, S, tm, tn):
    T, D = h.shape
    N = w.shape[1]
    tpb = S // tm
    in_specs = [pl.BlockSpec((tm, D), lambda i, j: (i, 0)),
                pl.BlockSpec((1, D), lambda i, j: (0, 0)),
                pl.BlockSpec((1, 1, D), lambda i, j: (i // tpb, 0, 0)),
                pl.BlockSpec((1, 1, D), lambda i, j: (i // tpb, 0, 0)),
                pl.BlockSpec((D, tn), lambda i, j: (0, j))]
    out_specs = [pl.BlockSpec((tm, tn), lambda i, j: (i, j))]
    out_shape = [jax.ShapeDtypeStruct((T, N), f32)]
    args = [h, nw, scale, shift, w]
    if w_small is not None:
        ns = w_small.shape[1]
        in_specs.append(pl.BlockSpec((D, ns), lambda i, j: (0, 0)))
        out_specs.append(pl.BlockSpec((tm, ns), lambda i, j: (i, 0)))
        out_shape.append(jax.ShapeDtypeStruct((T, ns), f32))
        args.append(w_small)
    res = pl.pallas_call(
        functools.partial(_norm_mm_body, has_small=w_small is not None),
        grid=(T // tm, N // tn),
        in_specs=in_specs, out_specs=out_specs, out_shape=out_shape,
        scratch_shapes=[pltpu.VMEM((tm, D), bf16)],
        compiler_params=_params("arbitrary", "arbitrary"),
        name="norm_matmul",
    )(*args)
    return res if w_small is not None else res[0]


def _mm_res_body(a_ref, w_ref, h_ref, g_ref, o_ref):
    acc = jnp.dot(a_ref[...], w_ref[...], preferred_element_type=f32)
    o_ref[...] = h_ref[...] + g_ref[0] * acc


def _matmul_residual(a, w, h, gate, S, tm, tn):
    T, K = a.shape
    D = w.shape[1]
    tpb = S // tm
    return pl.pallas_call(
        _mm_res_body,
        grid=(T // tm, D // tn),
        in_specs=[pl.BlockSpec((tm, K), lambda i, j: (i, 0)),
                  pl.BlockSpec((K, tn), lambda i, j: (0, j)),
                  pl.BlockSpec((tm, tn), lambda i, j: (i, j)),
                  pl.BlockSpec((1, 1, tn), lambda i, j: (i // tpb, 0, j))],
        out_specs=pl.BlockSpec((tm, tn), lambda i, j: (i, j)),
        out_shape=jax.ShapeDtypeStruct((T, D), f32),
        compiler_params=_params("arbitrary", "arbitrary"),
        name="matmul_residual",
    )(a, w, h, gate)


def _causal_conv(x_ref, pad_ref, cw_ref, tt):
    pad_ref[SUBLANES:SUBLANES + tt, :] = x_ref[...].astype(f32)
    base = SUBLANES - (CONV_WIDTH - 1)
    acc = cw_ref[0:1, :] * pad_ref[pl.ds(base, tt), :]
    for k in range(1, CONV_WIDTH):
        acc = acc + cw_ref[k:k + 1, :] * pad_ref[pl.ds(base + k, tt), :]
    pad_ref[0:SUBLANES, :] = pad_ref[tt:tt + SUBLANES, :]
    return acc


def _lru_body(xb_ref, yb_ref, cw_ref, cb_ref, wg_ref, brg_ref, big_ref, lam_ref, o_ref,
              pad_ref, hstate_ref, a_ref, b_ref, hs_ref, *, tt):
    @pl.when(pl.program_id(1) == 0)
    def _():
        pad_ref[0:SUBLANES, :] = jnp.zeros((SUBLANES, pad_ref.shape[1]), f32)
        hstate_ref[...] = jnp.zeros_like(hstate_ref)

    xc = _causal_conv(xb_ref, pad_ref, cw_ref, tt) + cb_ref[...]
    neg_c_sp = -LRU_C * _softplus(-lam_ref[...])
    hd = LRU_HEAD_DIM
    for hh in range(LRU_HEADS):
        sl = slice(hh * hd, (hh + 1) * hd)
        xh = xc[:, sl]
        gates = _dot(xh, wg_ref[hh])
        r = jax.nn.sigmoid(gates[:, :hd] + brg_ref[:, sl])
        ig = jax.nn.sigmoid(gates[:, hd:] + big_ref[:, sl])
        a = jnp.exp(r * neg_c_sp[:, sl])
        a_ref[:, sl] = a
        b_ref[:, sl] = jnp.sqrt(jnp.maximum(1.0 - a * a, 0.0)) * (ig * xh)

    def step(i, h):
        h = a_ref[pl.ds(i, 1), :] * h + b_ref[pl.ds(i, 1), :]
        hs_ref[pl.ds(i, 1), :] = h
        return h

    hstate_ref[...] = lax.fori_loop(0, tt, step, hstate_ref[...], unroll=8)
    o_ref[...] = (hs_ref[...] * jax.nn.gelu(yb_ref[...].astype(f32), approximate=True)).astype(o_ref.dtype)


def _lru_core(proj, conv_w, conv_b, w_gates, b_rg, b_ig, lam, B, S, tt):
    T = proj.shape[0]
    D = proj.shape[1] // 2
    nt = S // tt
    vec = pl.BlockSpec((1, D), lambda b, t: (0, 0))
    return pl.pallas_call(
        functools.partial(_lru_body, tt=tt),
        grid=(B, nt),
        in_specs=[pl.BlockSpec((tt, D), lambda b, t: (b * nt + t, 0)),
                  pl.BlockSpec((tt, D), lambda b, t: (b * nt + t, 1)),
                  pl.BlockSpec((CONV_WIDTH, D), lambda b, t: (0, 0)),
                  vec,
                  pl.BlockSpec((LRU_HEADS, LRU_HEAD_DIM, 2 * LRU_HEAD_DIM), lambda b, t: (0, 0, 0)),
                  vec, vec, vec],
        out_specs=pl.BlockSpec((tt, D), lambda b, t: (b * nt + t, 0)),
        out_shape=jax.ShapeDtypeStruct((T, D), bf16),
        scratch_shapes=[pltpu.VMEM((tt + SUBLANES, D), f32),
                        pltpu.VMEM((1, D), f32),
                        pltpu.VMEM((tt, D), f32),
                        pltpu.VMEM((tt, D), f32),
                        pltpu.VMEM((tt, D), f32)],
        compiler_params=_params("arbitrary", "arbitrary"),
        name="lru_core",
    )(proj, proj, conv_w, conv_b, w_gates, b_rg, b_ig, lam)


def _gdn_gate_body(ba_ref, alog_ref, dtb_ref, beta_ref, gc_ref, *, tt):
    nh = GDN_VALUE_HEADS
    ba = ba_ref[...]
    beta_ref[...] = jax.nn.sigmoid(ba[:, :nh])
    g = -jnp.exp(alog_ref[...]) * _softplus(ba[:, nh:] + dtb_ref[...])
    ri = lax.broadcasted_iota(i32, (tt, tt), 0)
    ci = lax.broadcasted_iota(i32, (tt, tt), 1)
    same_chunk = (ri // CHUNK) == (ci // CHUNK)
    tri = jnp.where(same_chunk & (ci <= ri), 1.0, 0.0).astype(f32)
    gc_ref[...] = jnp.dot(tri, g, precision=lax.Precision.HIGHEST, preferred_element_type=f32)


def _gdn_gates(ba, a_log, dt_bias, tt):
    T = ba.shape[0]
    nh = GDN_VALUE_HEADS
    out = jax.ShapeDtypeStruct((T, nh), f32)
    return pl.pallas_call(
        functools.partial(_gdn_gate_body, tt=tt),
        grid=(T // tt,),
        in_specs=[pl.BlockSpec((tt, 2 * nh), lambda i: (i, 0)),
                  pl.BlockSpec((1, nh), lambda i: (0, 0)),
                  pl.BlockSpec((1, nh), lambda i: (0, 0))],
        out_specs=[pl.BlockSpec((tt, nh), lambda i: (i, 0)),
                   pl.BlockSpec((tt, nh), lambda i: (i, 0))],
        out_shape=[out, out],
        compiler_params=_params("arbitrary"),
        name="gdn_gates",
    )(ba, a_log, dt_bias)


def _gdn_body(q_ref, k_ref, v_ref, z_ref, cwq_ref, cwk_ref, cwv_ref, gc_ref, be_ref, nw_ref, o_ref,
              qpad_ref, kpad_ref, vpad_ref, state_ref, *, G, tt):
    hd = GDN_HEAD_DIM
    group = GDN_VALUE_HEADS // GDN_KEY_HEADS

    @pl.when(pl.program_id(2) == 0)
    def _():
        for p in (qpad_ref, kpad_ref, vpad_ref):
            p[0:SUBLANES, :] = jnp.zeros((SUBLANES, p.shape[1]), f32)
        state_ref[...] = jnp.zeros_like(state_ref)

    q = _silu(_causal_conv(q_ref, qpad_ref, cwq_ref, tt))
    k = _silu(_causal_conv(k_ref, kpad_ref, cwk_ref, tt))
    v = _silu(_causal_conv(v_ref, vpad_ref, cwv_ref, tt))
    gcb = gc_ref[...]
    beb = be_ref[...]
    lane = lax.broadcasted_iota(i32, gcb.shape, 1)
    ri = lax.broadcasted_iota(i32, (CHUNK, CHUNK), 0)
    ci = lax.broadcasted_iota(i32, (CHUNK, CHUNK), 1)
    causal = ri >= ci
    strict = ri > ci
    eye = ri == ci
    nchunk = tt // CHUNK

    for kh in range(G):
        qh = q[:, kh * hd:(kh + 1) * hd]
        kk = k[:, kh * hd:(kh + 1) * hd]
        qn = qh * (lax.rsqrt(jnp.sum(qh * qh, axis=-1, keepdims=True) + L2_EPS) * (hd ** -0.5))
        kn = kk * lax.rsqrt(jnp.sum(kk * kk, axis=-1, keepdims=True) + L2_EPS)
        kk_c, qk_c = [], []
        for c in range(nchunk):
            rows = slice(c * CHUNK, (c + 1) * CHUNK)
            kk_c.append(_dot_nt(kn[rows], kn[rows]))
            qk_c.append(_dot_nt(qn[rows], kn[rows]))
        for vh in range(group):
            hl = kh * group + vh
            hidx = pl.program_id(1) * (G * group) + hl
            g_all = jnp.sum(jnp.where(lane == hidx, gcb, 0.0), axis=1, keepdims=True)
            b_all = jnp.sum(jnp.where(lane == hidx, beb, 0.0), axis=1, keepdims=True)
            vv = v[:, hl * hd:(hl + 1) * hd]
            st = state_ref[hl]
            outs = []
            for c in range(nchunk):
                rows = slice(c * CHUNK, (c + 1) * CHUNK)
                qc, kc, vc = qn[rows], kn[rows], vv[rows]
                gcol, bcol = g_all[rows], b_all[rows]
                grow = jnp.sum(jnp.where(eye, gcol, 0.0), axis=0, keepdims=True)
                decay = jnp.where(causal, jnp.exp(gcol - grow), 0.0)
                a_mat = jnp.where(strict, bcol * kk_c[c] * decay, 0.0)
                eg = jnp.exp(gcol)
                x = jnp.concatenate([vc * bcol, kc * (bcol * eg)], axis=1)
                x = x - _dot(a_mat, x)
                p = a_mat
                span = 2
                while span < CHUNK:
                    p = _dot(p, p)
                    x = x + _dot(p, x)
                    span *= 2
                u_val, w_dec = x[:, :hd], x[:, hd:]
                v_new = u_val - _dot(w_dec, st)
                o = _dot(qc * eg, st) + _dot(qk_c[c] * decay, v_new)
                glast = gcol[CHUNK - 1:CHUNK, :]
                k_dec = kc * jnp.exp(glast - gcol)
                st = st * jnp.exp(glast) + _dot_tn(k_dec, v_new)
                outs.append(o)
            state_ref[hl] = st
            o_all = jnp.concatenate(outs, axis=0)
            zz = z_ref[:, hl * hd:(hl + 1) * hd].astype(f32)
            on = o_all * lax.rsqrt(jnp.mean(o_all * o_all, axis=-1, keepdims=True) + RMS_EPS) * nw_ref[...]
            o_ref[:, hl * hd:(hl + 1) * hd] = (on * _silu(zz)).astype(o_ref.dtype)


def _gdn_core(pm, conv_w, gc, beta, norm_w, B, S, tt, G):
    T = pm.shape[0]
    hd = GDN_HEAD_DIM
    nt = S // tt
    qw = G * hd
    vw = 2 * G * hd
    nq = GDN_QK_DIM // qw
    nv = GDN_V_DIM // vw
    row = lambda b, g, t: b * nt + t
    return pl.pallas_call(
        functools.partial(_gdn_body, G=G, tt=tt),
        grid=(B, GDN_KEY_HEADS // G, nt),
        in_specs=[pl.BlockSpec((tt, qw), lambda b, g, t: (row(b, g, t), g)),
                  pl.BlockSpec((tt, qw), lambda b, g, t: (row(b, g, t), nq + g)),
                  pl.BlockSpec((tt, vw), lambda b, g, t: (row(b, g, t), nv + g)),
                  pl.BlockSpec((tt, vw), lambda b, g, t: (row(b, g, t), 2 * nv + g)),
                  pl.BlockSpec((CONV_WIDTH, qw), lambda b, g, t: (0, g)),
                  pl.BlockSpec((CONV_WIDTH, qw), lambda b, g, t: (0, nq + g)),
                  pl.BlockSpec((CONV_WIDTH, vw), lambda b, g, t: (0, nv + g)),
                  pl.BlockSpec((tt, GDN_VALUE_HEADS), lambda b, g, t: (row(b, g, t), 0)),
                  pl.BlockSpec((tt, GDN_VALUE_HEADS), lambda b, g, t: (row(b, g, t), 0)),
                  pl.BlockSpec((1, hd), lambda b, g, t: (0, 0))],
        out_specs=pl.BlockSpec((tt, vw), lambda b, g, t: (row(b, g, t), g)),
        out_shape=jax.ShapeDtypeStruct((T, GDN_V_DIM), bf16),
        scratch_shapes=[pltpu.VMEM((tt + SUBLANES, qw), f32),
                        pltpu.VMEM((tt + SUBLANES, qw), f32),
                        pltpu.VMEM((tt + SUBLANES, vw), f32),
                        pltpu.VMEM((2 * G, hd, hd), f32)],
        compiler_params=_params("arbitrary", "arbitrary", "arbitrary"),
        name="gdn_core",
    )(pm, pm, pm, pm, conv_w, conv_w, conv_w, gc, beta, norm_w)


def _router_body(h_ref, nw_ref, sc_ref, sh_ref, wr_ref, br_ref,
                 u_ref, ids_ref, wts_ref, rk_ref, cnt_ref, *, tm):
    @pl.when(pl.program_id(0) == 0)
    def _():
        cnt_ref[...] = jnp.zeros_like(cnt_ref)

    u = _modulated_norm(h_ref[...], nw_ref[...], sc_ref[0], sh_ref[0])
    u_ref[...] = u
    logits = lax.dot_general(wr_ref[...], u, (((1,), (1,)), ((), ())),
                             precision=lax.Precision.HIGHEST,
                             preferred_element_type=f32) + br_ref[...]
    eidx = lax.broadcasted_iota(i32, logits.shape, 0)
    cur = logits
    vals, sels = [], []
    for _ in range(TOP_K):
        m = jnp.max(cur, axis=0, keepdims=True)
        sel = jnp.min(jnp.where(cur == m, eidx, N_EXPERTS), axis=0, keepdims=True)
        vals.append(m)
        sels.append(sel)
        cur = jnp.where(eidx == sel, -jnp.inf, cur)
    ex = [jnp.exp(v - vals[0]) for v in vals]
    den = ex[0] + ex[1] + ex[2] + ex[3]
    wts_ref[...] = jnp.concatenate([e / den for e in ex], axis=0)
    ids_ref[...] = jnp.concatenate(sels, axis=0)
    onehots = [eidx == s for s in sels]
    chosen = jnp.zeros(logits.shape, f32)
    for oh in onehots:
        chosen = chosen + jnp.where(oh, 1.0, 0.0)
    ti = lax.broadcasted_iota(i32, (tm, tm), 0)
    tj = lax.broadcasted_iota(i32, (tm, tm), 1)
    before = jnp.where(ti < tj, 1.0, 0.0)
    base = _dot(chosen, before) + cnt_ref[:, 0:1]
    ranks = [jnp.sum(jnp.where(oh, base, 0.0), axis=0, keepdims=True) for oh in onehots]
    rk_ref[...] = jnp.concatenate(ranks, axis=0).astype(i32)
    cnt_ref[...] = cnt_ref[...] + jnp.sum(chosen, axis=1, keepdims=True)


def _router(h, nw, scale, shift, w_r_t, b_r, S, tm):
    T, D = h.shape
    tpb = S // tm
    lanes = 128
    return pl.pallas_call(
        functools.partial(_router_body, tm=tm),
        grid=(T // tm,),
        in_specs=[pl.BlockSpec((tm, D), lambda i: (i, 0)),
                  pl.BlockSpec((1, D), lambda i: (0, 0)),
                  pl.BlockSpec((1, 1, D), lambda i: (i // tpb, 0, 0)),
                  pl.BlockSpec((1, 1, D), lambda i: (i // tpb, 0, 0)),
                  pl.BlockSpec((N_EXPERTS, D), lambda i: (0, 0)),
                  pl.BlockSpec((N_EXPERTS, 1), lambda i: (0, 0))],
        out_specs=[pl.BlockSpec((tm, D), lambda i: (i, 0)),
                   pl.BlockSpec((TOP_K, tm), lambda i: (0, i)),
                   pl.BlockSpec((TOP_K, tm), lambda i: (0, i)),
                   pl.BlockSpec((TOP_K, tm), lambda i: (0, i)),
                   pl.BlockSpec((N_EXPERTS, lanes), lambda i: (0, 0))],
        out_shape=[jax.ShapeDtypeStruct((T, D), f32),
                   jax.ShapeDtypeStruct((TOP_K, T), i32),
                   jax.ShapeDtypeStruct((TOP_K, T), f32),
                   jax.ShapeDtypeStruct((TOP_K, T), i32),
                   jax.ShapeDtypeStruct((N_EXPERTS, lanes), f32)],
        compiler_params=_params("arbitrary"),
        name="moe_router",
    )(h, nw, scale, shift, w_r_t, b_r)


def _gather_pipeline(idx_hbm, src_hbm, idx_smem, buf, isem, gsem, n_idx):
    i = pl.program_id(0)
    nblk = pl.num_programs(0)
    slot = lax.rem(i, 2)
    nslot = 1 - slot

    def idx_copy(blk, s):
        return pltpu.make_async_copy(idx_hbm.at[blk], idx_smem.at[s], isem.at[s])

    def row_copy(s, r, row):
        return pltpu.make_async_copy(src_hbm.at[pl.ds(row, 1), :], buf.at[s, pl.ds(r, 1), :], gsem.at[s])

    def issue_rows(s):
        def body(r, carry):
            row_copy(s, r, idx_smem[s, 0, r]).start()
            return carry
        lax.fori_loop(0, n_idx, body, 0, unroll=8)

    def wait_rows(s):
        def body(r, carry):
            row_copy(s, r, 0).wait()
            return carry
        lax.fori_loop(0, n_idx, body, 0, unroll=8)

    @pl.when(i == 0)
    def _():
        first = idx_copy(0, 0)
        first.start()
        first.wait()
        issue_rows(0)

        @pl.when(nblk > 1)
        def _():
            idx_copy(1, 1).start()

    @pl.when(i + 1 < nblk)
    def _():
        idx_copy(i + 1, nslot).wait()
        issue_rows(nslot)

    @pl.when(i + 2 < nblk)
    def _():
        idx_copy(i + 2, slot).start()

    wait_rows(slot)
    return slot


def _moe_body(be_ref, nu_ref, rt_hbm, u_hbm, wgu_ref, bgu_ref, wd_ref, bd_ref, y_ref,
              idx_smem, xbuf, isem, gsem):
    slot = _gather_pipeline(rt_hbm, u_hbm, idx_smem, xbuf, isem, gsem, MOE_ROWS)
    used = pl.program_id(0) < nu_ref[0]

    @pl.when(used)
    def _():
        dff = wd_ref.shape[1]
        gu = _dot(xbuf[slot], wgu_ref[0]) + bgu_ref[0]
        glu = jnp.minimum(gu[:, :dff], SWIGLU_LIMIT)
        lin = jnp.clip(gu[:, dff:], -SWIGLU_LIMIT, SWIGLU_LIMIT)
        hid = glu * jax.nn.sigmoid(SWIGLU_ALPHA * glu) * (lin + 1.0)
        y_ref[...] = _dot(hid, wd_ref[0]) + bd_ref[0]

    @pl.when(jnp.logical_not(used))
    def _():
        y_ref[...] = jnp.zeros_like(y_ref)


def _moe_experts(blk_expert, n_used, row_tok, u, w_gu, b_gu, w_down, b_down):
    nblk = row_tok.shape[0]
    T, D = u.shape
    dff2 = w_gu.shape[2]
    dff = w_down.shape[1]
    grid_spec = pltpu.PrefetchScalarGridSpec(
        num_scalar_prefetch=2,
        grid=(nblk,),
        in_specs=[pl.BlockSpec(memory_space=pl.ANY),
                  pl.BlockSpec(memory_space=pl.ANY),
                  pl.BlockSpec((1, D, dff2), lambda i, be, nu: (be[i], 0, 0)),
                  pl.BlockSpec((1, 1, dff2), lambda i, be, nu: (be[i], 0, 0)),
                  pl.BlockSpec((1, dff, D), lambda i, be, nu: (be[i], 0, 0)),
                  pl.BlockSpec((1, 1, D), lambda i, be, nu: (be[i], 0, 0))],
        out_specs=pl.BlockSpec((MOE_ROWS, D), lambda i, be, nu: (i, 0)),
        scratch_shapes=[pltpu.SMEM((2, 1, MOE_ROWS), i32),
                        pltpu.VMEM((2, MOE_ROWS, D), f32),
                        pltpu.SemaphoreType.DMA((2,)),
                        pltpu.SemaphoreType.DMA((2,))],
    )
    return pl.pallas_call(
        _moe_body,
        grid_spec=grid_spec,
        out_shape=jax.ShapeDtypeStruct((nblk * MOE_ROWS, D), f32),
        compiler_params=_params("arbitrary"),
        name="moe_experts",
    )(blk_expert, n_used, row_tok, u, w_gu, b_gu, w_down, b_down)


def _combine_body(d_hbm, y_hbm, h_ref, w_ref, g_ref, o_ref, idx_smem, ybuf, isem, gsem, *, tm):
    slot = _gather_pipeline(d_hbm, y_hbm, idx_smem, ybuf, isem, gsem, TOP_K * tm)
    w = w_ref[...]
    mix = w[:, 0:1] * ybuf[slot, 0:tm, :]
    for k in range(1, TOP_K):
        mix = mix + w[:, k:k + 1] * ybuf[slot, k * tm:(k + 1) * tm, :]
    o_ref[...] = h_ref[...] + g_ref[0] * mix


def _moe_combine(dest_blk, y, h, wts_t, gate, S, tm):
    T, D = h.shape
    tpb = S // tm
    return pl.pallas_call(
        functools.partial(_combine_body, tm=tm),
        grid=(T // tm,),
        in_specs=[pl.BlockSpec(memory_space=pl.ANY),
                  pl.BlockSpec(memory_space=pl.ANY),
                  pl.BlockSpec((tm, D), lambda i: (i, 0)),
                  pl.BlockSpec((tm, TOP_K), lambda i: (i, 0)),
                  pl.BlockSpec((1, 1, D), lambda i: (i // tpb, 0, 0))],
        out_specs=pl.BlockSpec((tm, D), lambda i: (i, 0)),
        out_shape=jax.ShapeDtypeStruct((T, D), f32),
        scratch_shapes=[pltpu.SMEM((2, 1, TOP_K * tm), i32),
                        pltpu.VMEM((2, TOP_K * tm, D), f32),
                        pltpu.SemaphoreType.DMA((2,)),
                        pltpu.SemaphoreType.DMA((2,))],
        compiler_params=_params("arbitrary"),
        name="moe_combine",
    )(dest_blk, y, h, wts_t, gate)


def _moe_layer(h, nw, scale, shift, gate, w_router, b_router, w_gu, b_gu, w_down, b_down, S):
    T, D = h.shape
    tm_r = min(512, S)
    u, ids, wts, ranks, cnt = _router(h, nw, scale, shift, w_router.T, b_router.reshape(N_EXPERTS, 1),
                                      S, tm_r)
    counts = cnt[:, 0].astype(i32)
    padded = (counts + MOE_ROWS - 1) // MOE_ROWS * MOE_ROWS
    pad_end = jnp.cumsum(padded)
    pad_start = pad_end - padded
    dest = pad_start[ids] + ranks
    nblk = -(-(T * TOP_K + N_EXPERTS * (MOE_ROWS - 1)) // MOE_ROWS)
    tok = jnp.broadcast_to(jnp.arange(T, dtype=i32)[None, :], (TOP_K, T))
    row_tok = jnp.zeros((nblk * MOE_ROWS,), i32).at[dest.reshape(-1)].set(tok.reshape(-1))
    blk_start = jnp.arange(nblk, dtype=i32) * MOE_ROWS
    blk_expert = jnp.minimum(jnp.searchsorted(pad_end, blk_start, side="right"),
                             N_EXPERTS - 1).astype(i32)
    n_used = (pad_end[-1:] // MOE_ROWS).astype(i32)
    y = _moe_experts(blk_expert, n_used, row_tok.reshape(nblk, 1, MOE_ROWS), u,
                     w_gu.astype(bf16), b_gu.reshape(N_EXPERTS, 1, -1),
                     w_down.astype(bf16), b_down.reshape(N_EXPERTS, 1, -1))
    tm_c = min(256, S)
    dest_blk = dest.reshape(TOP_K, T // tm_c, tm_c).transpose(1, 0, 2).reshape(T // tm_c, 1, TOP_K * tm_c)
    return _moe_combine(dest_blk, y, h, wts.T, gate, S, tm_c)


def _final_norm_body(h_ref, nw_ref, o_ref):
    x = h_ref[...]
    ms = jnp.mean(x * x, axis=-1, keepdims=True)
    o_ref[...] = x * lax.rsqrt(ms + RMS_EPS) * nw_ref[...]


def _final_norm(h, nw, tm):
    T, D = h.shape
    return pl.pallas_call(
        _final_norm_body,
        grid=(T // tm,),
        in_specs=[pl.BlockSpec((tm, D), lambda i: (i, 0)),
                  pl.BlockSpec((1, D), lambda i: (0, 0))],
        out_specs=pl.BlockSpec((tm, D), lambda i: (i, 0)),
        out_shape=jax.ShapeDtypeStruct((T, D), f32),
        compiler_params=_params("arbitrary"),
        name="final_norm",
    )(h, nw)


def kernel(x, c, w_ada, ada_table, norm_mix, norm_ffn, norm_final, lru_w_in, lru_conv_w, lru_conv_b, lru_w_rg, lru_b_rg, lru_w_ig, lru_b_ig, lru_lambda, lru_w_out, gdn_w_in, gdn_conv_w, gdn_a_log, gdn_dt_bias, gdn_norm, gdn_w_out, moe_w_router, moe_b_router, moe_w_gate_up, moe_b_gate_up, moe_w_down, moe_b_down):
    B, S, D = x.shape
    T = B * S
    depth = ada_table.shape[0]
    tm = min(1024, S)
    tt_lru = min(256, S)
    tt_gdn = min(256, S)

    cond = _ada(c, w_ada).reshape(B, N_MODS, D)
    h = x.reshape(T, D)
    for layer in range(depth):
        mods = cond + ada_table[layer]
        shift_m, scale_m, gate_m, shift_f, scale_f, gate_f = [mods[:, j:j + 1, :] for j in range(N_MODS)]
        nw = norm_mix[layer].reshape(1, D)
        j = layer // 2
        if layer % 2 == 0:
            proj = _norm_matmul(h, nw, scale_m, shift_m, lru_w_in[j].astype(bf16), None, S, tm, 1024)
            w_gates = jnp.concatenate([lru_w_rg[j], lru_w_ig[j]], axis=-1).astype(bf16)
            act = _lru_core(proj, lru_conv_w[j], lru_conv_b[j].reshape(1, D), w_gates,
                            lru_b_rg[j].reshape(1, D), lru_b_ig[j].reshape(1, D),
                            lru_lambda[j].reshape(1, D), B, S, tt_lru)
            h = _matmul_residual(act, lru_w_out[j].astype(bf16), h, gate_m, S, tm, 512)
        else:
            w_in = gdn_w_in[j]
            n_main = GDN_CONV_DIM + GDN_V_DIM
            pm, ba = _norm_matmul(h, nw, scale_m, shift_m, w_in[:, :n_main].astype(bf16),
                                  w_in[:, n_main:].astype(bf16), S, tm, 1024)
            beta, gc = _gdn_gates(ba, gdn_a_log[j].reshape(1, -1), gdn_dt_bias[j].reshape(1, -1),
                                  min(512, S))
            act = _gdn_core(pm, gdn_conv_w[j], gc, beta, gdn_norm[j].reshape(1, -1), B, S, tt_gdn, 2)
            h = _matmul_residual(act, gdn_w_out[j].astype(bf16), h, gate_m, S, tm, 512)
        h = _moe_layer(h, norm_ffn[layer].reshape(1, D), scale_f, shift_f, gate_f,
                       moe_w_router[layer], moe_b_router[layer], moe_w_gate_up[layer],
                       moe_b_gate_up[layer], moe_w_down[layer], moe_b_down[layer], S)
    return _final_norm(h, norm_final.reshape(1, D), tm).reshape(B, S, D)
```

```python
import functools

import jax
import jax.numpy as jnp
from jax import lax
from jax.experimental import pallas as pl
from jax.experimental.pallas import tpu as pltpu

f32 = jnp.float32
bf16 = jnp.bfloat16
i32 = jnp.int32

N_MODS = 6
CONV_WIDTH = 4
RMS_EPS = 1e-6
L2_EPS = 1e-6

LRU_HEADS = 16
LRU_HEAD_DIM = 128
LRU_C = 8.0

GDN_HEAD_DIM = 128
GDN_KEY_HEADS = 16
GDN_VALUE_HEADS = 32
GDN_QK_DIM = GDN_KEY_HEADS * GDN_HEAD_DIM
GDN_V_DIM = GDN_VALUE_HEADS * GDN_HEAD_DIM
GDN_CONV_DIM = 2 * GDN_QK_DIM + GDN_V_DIM
CHUNK = 64

N_EXPERTS = 32
TOP_K = 4
SWIGLU_ALPHA = 1.702
SWIGLU_LIMIT = 7.0
MOE_ROWS = 256

VMEM_LIMIT_BYTES = 56 * 1024 * 1024
SUBLANES = 8


def _params(*sem):
    return pltpu.CompilerParams(dimension_semantics=sem, vmem_limit_bytes=VMEM_LIMIT_BYTES)


def _dot(a, b):
    return jnp.dot(a.astype(bf16), b.astype(bf16), preferred_element_type=f32)


def _dot_nt(a, b):
    return lax.dot_general(a.astype(bf16), b.astype(bf16), (((1,), (1,)), ((), ())),
                           preferred_element_type=f32)


def _dot_tn(a, b):
    return lax.dot_general(a.astype(bf16), b.astype(bf16), (((0,), (0,)), ((), ())),
                           preferred_element_type=f32)


def _silu(x):
    return x * jax.nn.sigmoid(x)


def _softplus(x):
    return jnp.maximum(x, 0.0) + jnp.log(1.0 + jnp.exp(-jnp.abs(x)))


def _modulated_norm(x, nw, scale, shift):
    ms = jnp.mean(x * x, axis=-1, keepdims=True)
    y = x * lax.rsqrt(ms + RMS_EPS) * nw
    return y * (1.0 + scale) + shift


def _ada_body(c_ref, w_ref, o_ref):
    o_ref[...] = _dot(_silu(c_ref[...]), w_ref[...])


def _ada(c, w_ada):
    B, D = c.shape
    N = w_ada.shape[1]
    tn = 1024
    return pl.pallas_call(
        _ada_body,
        grid=(N // tn,),
        in_specs=[pl.BlockSpec((B, D), lambda j: (0, 0)),
                  pl.BlockSpec((D, tn), lambda j: (0, j))],
        out_specs=pl.BlockSpec((B, tn), lambda j: (0, j)),
        out_shape=jax.ShapeDtypeStruct((B, N), f32),
        compiler_params=_params("arbitrary"),
        name="ada",
    )(c, w_ada)


def _norm_mm_body(h_ref, nw_ref, sc_ref, sh_ref, w_ref, *rest, has_small):
    if has_small:
        ws_ref, o_ref, os_ref, u_ref = rest
    else:
        o_ref, u_ref = rest

    @pl.when(pl.program_id(1) == 0)
    def _():
        u = _modulated_norm(h_ref[...], nw_ref[...], sc_ref[0], sh_ref[0]).astype(bf16)
        u_ref[...] = u
        if has_small:
            os_ref[...] = jnp.dot(u, ws_ref[...], preferred_element_type=f32)

    o_ref[...] = jnp.dot(u_ref[...], w_ref[...], preferred_element_type=f32).astype(o_ref.dtype)


def _norm_matmul(h, nw, scale, shift, w, w_small, S, tm, tn):
    T, D = h.shape
    N = w.shape[1]
    tpb = S // tm
    in_specs = [pl.BlockSpec((tm, D), lambda i, j: (i, 0)),
                pl.BlockSpec((1, D), lambda i, j: (0, 0)),
                pl.BlockSpec((1, 1, D), lambda i, j: (i // tpb, 0, 0)),
                pl.BlockSpec((1, 1, D), lambda i, j: (i // tpb, 0, 0)),
                pl.BlockSpec((D, tn), lambda i, j: (0, j))]
    out_specs = [pl.BlockSpec((tm, tn), lambda i, j: (i, j))]
    out_shape = [jax.ShapeDtypeStruct((T, N), f32)]
    args = [h, nw, scale, shift, w]
    if w_small is not None:
        ns = w_small.shape[1]
        in_specs.append(pl.BlockSpec((D, ns), lambda i, j: (0, 0)))
        out_specs.append(pl.BlockSpec((tm, ns), lambda i, j: (i, 0)))
        out_shape.append(jax.ShapeDtypeStruct((T, ns), f32))
        args.append(w_small)
    res = pl.pallas_call(
        functools.partial(_norm_mm_body, has_small=w_small is not None),
        grid=(T // tm, N // tn),
        in_specs=in_specs, out_specs=out_specs, out_shape=out_shape,
        scratch_shapes=[pltpu.VMEM((tm, D), bf16)],
        compiler_params=_params("arbitrary", "arbitrary"),
        name="norm_matmul",
    )(*args)
    return res if w_small is not None else res[0]


def _mm_res_body(a_ref, w_ref, h_ref, g_ref, o_ref):
    acc = jnp.dot(a_ref[...], w_ref[...], preferred_element_type=f32)
    o_ref[...] = h_ref[...] + g_ref[0] * acc


def _matmul_residual(a, w, h, gate, S, tm, tn):
    T, K = a.shape
    D = w.shape[1]
    tpb = S // tm
    return pl.pallas_call(
        _mm_res_body,
        grid=(T // tm, D // tn),
        in_specs=[pl.BlockSpec((tm, K), lambda i, j: (i, 0)),
                  pl.BlockSpec((K, tn), lambda i, j: (0, j)),
                  pl.BlockSpec((tm, tn), lambda i, j: (i, j)),
                  pl.BlockSpec((1, 1, tn), lambda i, j: (i // tpb, 0, j))],
        out_specs=pl.BlockSpec((tm, tn), lambda i, j: (i, j)),
        out_shape=jax.ShapeDtypeStruct((T, D), f32),
        compiler_params=_params("arbitrary", "arbitrary"),
        name="matmul_residual",
    )(a, w, h, gate)


def _causal_conv(x_ref, pad_ref, cw_ref, tt):
    pad_ref[SUBLANES:SUBLANES + tt, :] = x_ref[...].astype(f32)
    base = SUBLANES - (CONV_WIDTH - 1)
    acc = cw_ref[0:1, :] * pad_ref[pl.ds(base, tt), :]
    for k in range(1, CONV_WIDTH):
        acc = acc + cw_ref[k:k + 1, :] * pad_ref[pl.ds(base + k, tt), :]
    pad_ref[0:SUBLANES, :] = pad_ref[tt:tt + SUBLANES, :]
    return acc


def _lru_body(xb_ref, yb_ref, cw_ref, cb_ref, wg_ref, brg_ref, big_ref, lam_ref, o_ref,
              pad_ref, hstate_ref, a_ref, b_ref, hs_ref, *, tt):
    @pl.when(pl.program_id(1) == 0)
    def _():
        pad_ref[0:SUBLANES, :] = jnp.zeros((SUBLANES, pad_ref.shape[1]), f32)
        hstate_ref[...] = jnp.zeros_like(hstate_ref)

    xc = _causal_conv(xb_ref, pad_ref, cw_ref, tt) + cb_ref[...]
    neg_c_sp = -LRU_C * _softplus(-lam_ref[...])
    hd = LRU_HEAD_DIM
    for hh in range(LRU_HEADS):
        sl = slice(hh * hd, (hh + 1) * hd)
        xh = xc[:, sl]
        gates = _dot(xh, wg_ref[hh])
        r = jax.nn.sigmoid(gates[:, :hd] + brg_ref[:, sl])
        ig = jax.nn.sigmoid(gates[:, hd:] + big_ref[:, sl])
        a = jnp.exp(r * neg_c_sp[:, sl])
        a_ref[:, sl] = a
        b_ref[:, sl] = jnp.sqrt(jnp.maximum(1.0 - a * a, 0.0)) * (ig * xh)

    def step(i, h):
        h = a_ref[pl.ds(i, 1), :] * h + b_ref[pl.ds(i, 1), :]
        hs_ref[pl.ds(i, 1), :] = h
        return h

    hstate_ref[...] = lax.fori_loop(0, tt, step, hstate_ref[...], unroll=8)
    o_ref[...] = (hs_ref[...] * jax.nn.gelu(yb_ref[...].astype(f32), approximate=True)).astype(o_ref.dtype)


def _lru_core(proj, conv_w, conv_b, w_gates, b_rg, b_ig, lam, B, S, tt):
    T = proj.shape[0]
    D = proj.shape[1] // 2
    nt = S // tt
    vec = pl.BlockSpec((1, D), lambda b, t: (0, 0))
    return pl.pallas_call(
        functools.partial(_lru_body, tt=tt),
        grid=(B, nt),
        in_specs=[pl.BlockSpec((tt, D), lambda b, t: (b * nt + t, 0)),
                  pl.BlockSpec((tt, D), lambda b, t: (b * nt + t, 1)),
                  pl.BlockSpec((CONV_WIDTH, D), lambda b, t: (0, 0)),
                  vec,
                  pl.BlockSpec((LRU_HEADS, LRU_HEAD_DIM, 2 * LRU_HEAD_DIM), lambda b, t: (0, 0, 0)),
                  vec, vec, vec],
        out_specs=pl.BlockSpec((tt, D), lambda b, t: (b * nt + t, 0)),
        out_shape=jax.ShapeDtypeStruct((T, D), bf16),
        scratch_shapes=[pltpu.VMEM((tt + SUBLANES, D), f32),
                        pltpu.VMEM((1, D), f32),
                        pltpu.VMEM((tt, D), f32),
                        pltpu.VMEM((tt, D), f32),
                        pltpu.VMEM((tt, D), f32)],
        compiler_params=_params("arbitrary", "arbitrary"),
        name="lru_core",
    )(proj, proj, conv_w, conv_b, w_gates, b_rg, b_ig, lam)


def _gdn_gate_body(ba_ref, alog_ref, dtb_ref, beta_ref, gc_ref, *, tt):
    nh = GDN_VALUE_HEADS
    ba = ba_ref[...]
    beta_ref[...] = jax.nn.sigmoid(ba[:, :nh])
    g = -jnp.exp(alog_ref[...]) * _softplus(ba[:, nh:] + dtb_ref[...])
    ri = lax.broadcasted_iota(i32, (tt, tt), 0)
    ci = lax.broadcasted_iota(i32, (tt, tt), 1)
    same_chunk = (ri // CHUNK) == (ci // CHUNK)
    tri = jnp.where(same_chunk & (ci <= ri), 1.0, 0.0).astype(f32)
    gc_ref[...] = jnp.dot(tri, g, precision=lax.Precision.HIGHEST, preferred_element_type=f32)


def _gdn_gates(ba, a_log, dt_bias, tt):
    T = ba.shape[0]
    nh = GDN_VALUE_HEADS
    out = jax.ShapeDtypeStruct((T, nh), f32)
    return pl.pallas_call(
        functools.partial(_gdn_gate_body, tt=tt),
        grid=(T // tt,),
        in_specs=[pl.BlockSpec((tt, 2 * nh), lambda i: (i, 0)),
                  pl.BlockSpec((1, nh), lambda i: (0, 0)),
                  pl.BlockSpec((1, nh), lambda i: (0, 0))],
        out_specs=[pl.BlockSpec((tt, nh), lambda i: (i, 0)),
                   pl.BlockSpec((tt, nh), lambda i: (i, 0))],
        out_shape=[out, out],
        compiler_params=_params("arbitrary"),
        name="gdn_gates",
    )(ba, a_log, dt_bias)


def _gdn_body(q_ref, k_ref, v_ref, z_ref, cwq_ref, cwk_ref, cwv_ref, gc_ref, be_ref, nw_ref, o_ref,
              qpad_ref, kpad_ref, vpad_ref, state_ref, *, G, tt):
    hd = GDN_HEAD_DIM
    group = GDN_VALUE_HEADS // GDN_KEY_HEADS

    @pl.when(pl.program_id(2) == 0)
    def _():
        for p in (qpad_ref, kpad_ref, vpad_ref):
            p[0:SUBLANES, :] = jnp.zeros((SUBLANES, p.shape[1]), f32)
        state_ref[...] = jnp.zeros_like(state_ref)

    q = _silu(_causal_conv(q_ref, qpad_ref, cwq_ref, tt))
    k = _silu(_causal_conv(k_ref, kpad_ref, cwk_ref, tt))
    v = _silu(_causal_conv(v_ref, vpad_ref, cwv_ref, tt))
    gcb = gc_ref[...]
    beb = be_ref[...]
    nchunk = tt // CHUNK
    heads = G * group
    pw_rows = group * CHUNK
    assert pw_rows == hd
    ri = lax.broadcasted_iota(i32, (pw_rows, pw_rows), 0)
    ci = lax.broadcasted_iota(i32, (pw_rows, pw_rows), 1)
    same_head = (ri // CHUNK) == (ci // CHUNK)
    causal = same_head & (ri >= ci)
    strict = same_head & (ri > ci)
    eye = ri == ci
    col_head = lax.broadcasted_iota(i32, (1, pw_rows), 1) // CHUNK

    ones_hd = jnp.ones((hd, hd), bf16)

    def row_sumsq(x):
        return jnp.dot((x * x).astype(bf16), ones_hd, preferred_element_type=f32)

    qn, kn = [], []
    for kh in range(G):
        qh = q[:, kh * hd:(kh + 1) * hd]
        kk = k[:, kh * hd:(kh + 1) * hd]
        qn.append(qh * (lax.rsqrt(row_sumsq(qh) + L2_EPS) * (hd ** -0.5)))
        kn.append(kk * lax.rsqrt(row_sumsq(kk) + L2_EPS))
    pick_shape = (GDN_VALUE_HEADS, heads * hd)
    picked_head = pl.program_id(1) * heads + lax.broadcasted_iota(i32, pick_shape, 1) // hd
    pick = jnp.where(lax.broadcasted_iota(i32, pick_shape, 0) == picked_head, 1.0, 0.0).astype(bf16)
    g_hi = gcb.astype(bf16)
    g_lo = (gcb - g_hi.astype(f32)).astype(bf16)
    g_rep = (jnp.dot(g_hi, pick, preferred_element_type=f32)
             + jnp.dot(g_lo, pick, preferred_element_type=f32))
    b_rep = jnp.dot(beb.astype(bf16), pick, preferred_element_type=f32)
    g_all = [g_rep[:, hl * hd:(hl + 1) * hd] for hl in range(heads)]
    b_all = [b_rep[:, hl * hd:(hl + 1) * hd] for hl in range(heads)]

    def stack_heads(per_head):
        return jnp.concatenate(per_head, axis=0)

    items = [(kh, c) for c in range(nchunk) for kh in range(G)]
    rows_of = lambda c: slice(c * CHUNK, (c + 1) * CHUNK)
    decay, a_mat, xs, q_dec, k_dec_t, s_gain, qk = {}, {}, {}, {}, {}, {}, {}
    for p in items:
        kh, c = p
        hs = [kh * group + j for j in range(group)]
        kc = kn[kh][rows_of(c)]
        k2 = stack_heads([kc] * group)
        q2 = stack_heads([qn[kh][rows_of(c)]] * group)
        kt = kc.T
        kt2 = jnp.concatenate([kt] * group, axis=1)
        gcol = stack_heads([g_all[h][rows_of(c)] for h in hs])
        bcol = stack_heads([b_all[h][rows_of(c)] for h in hs])
        grow = jnp.sum(jnp.where(eye, gcol, 0.0), axis=0, keepdims=True)
        decay[p] = jnp.where(causal, jnp.exp(gcol - grow), 0.0)
        a_mat[p] = jnp.where(strict, bcol * _dot(k2, kt2) * decay[p], 0.0)
        qk[p] = _dot(q2, kt2) * decay[p]
        eg = jnp.exp(gcol)
        q_dec[p] = q2 * eg
        v2 = stack_heads([v[:, h * hd:(h + 1) * hd][rows_of(c)] for h in hs])
        xs[p] = jnp.concatenate([v2 * bcol, k2 * (bcol * eg)], axis=1)
        lasts = [g_all[h][c * CHUNK + CHUNK - 1:(c + 1) * CHUNK, :] for h in hs]
        gl_row = lasts[0]
        for j in range(1, group):
            gl_row = jnp.where(col_head == j, lasts[j], gl_row)
        for j in range(group):
            s_gain[kh * group + j, c] = jnp.exp(lasts[j])
        kd = kt2 * jnp.exp(gl_row - grow)
        k_dec_t[p] = stack_heads([jnp.where(col_head == j, kd, 0.0) for j in range(group)])
    for p in items:
        xs[p] = xs[p] - _dot(a_mat[p], xs[p])
    pw = dict(a_mat)
    span = 2
    while span < CHUNK:
        for p in items:
            pw[p] = _dot(pw[p], pw[p])
        for p in items:
            xs[p] = xs[p] + _dot(pw[p], xs[p])
        span *= 2
    lhs, o_add, s_add = {}, {}, {}
    for p in items:
        kh, c = p
        pq = _dot(qk[p], xs[p])
        mk = _dot(k_dec_t[p], xs[p])
        q_eff = q_dec[p] - pq[:, hd:]
        for j in range(group):
            n = (kh * group + j, c)
            o_add[n] = pq[j * CHUNK:(j + 1) * CHUNK, :hd]
            s_add[n] = mk[j * hd:(j + 1) * hd, :hd]
            lhs[n] = jnp.concatenate([q_eff[j * CHUNK:(j + 1) * CHUNK], mk[j * hd:(j + 1) * hd, hd:]], axis=0)
    st = [state_ref[hl] for hl in range(heads)]
    outs = [[] for _ in range(heads)]
    for c in range(nchunk):
        for hl in range(heads):
            n = (hl, c)
            r = _dot(lhs[n], st[hl])
            outs[hl].append(r[:CHUNK] + o_add[n])
            st[hl] = st[hl] * s_gain[n] - r[CHUNK:] + s_add[n]
    for hl in range(heads):
        state_ref[hl] = st[hl]
        o_all = jnp.concatenate(outs[hl], axis=0)
        zz = z_ref[:, hl * hd:(hl + 1) * hd].astype(f32)
        on = o_all * lax.rsqrt(row_sumsq(o_all) * (1.0 / hd) + RMS_EPS) * nw_ref[...]
        o_ref[:, hl * hd:(hl + 1) * hd] = (on * _silu(zz)).astype(o_ref.dtype)


def _gdn_core(pm, conv_w, gc, beta, norm_w, B, S, tt, G):
    T = pm.shape[0]
    hd = GDN_HEAD_DIM
    nt = S // tt
    qw = G * hd
    vw = 2 * G * hd
    nq = GDN_QK_DIM // qw
    nv = GDN_V_DIM // vw
    row = lambda b, g, t: b * nt + t
    return pl.pallas_call(
        functools.partial(_gdn_body, G=G, tt=tt),
        grid=(B, GDN_KEY_HEADS // G, nt),
        in_specs=[pl.BlockSpec((tt, qw), lambda b, g, t: (row(b, g, t), g)),
                  pl.BlockSpec((tt, qw), lambda b, g, t: (row(b, g, t), nq + g)),
                  pl.BlockSpec((tt, vw), lambda b, g, t: (row(b, g, t), nv + g)),
                  pl.BlockSpec((tt, vw), lambda b, g, t: (row(b, g, t), 2 * nv + g)),
                  pl.BlockSpec((CONV_WIDTH, qw), lambda b, g, t: (0, g)),
                  pl.BlockSpec((CONV_WIDTH, qw), lambda b, g, t: (0, nq + g)),
                  pl.BlockSpec((CONV_WIDTH, vw), lambda b, g, t: (0, nv + g)),
                  pl.BlockSpec((tt, GDN_VALUE_HEADS), lambda b, g, t: (row(b, g, t), 0)),
                  pl.BlockSpec((tt, GDN_VALUE_HEADS), lambda b, g, t: (row(b, g, t), 0)),
                  pl.BlockSpec((1, hd), lambda b, g, t: (0, 0))],
        out_specs=pl.BlockSpec((tt, vw), lambda b, g, t: (row(b, g, t), g)),
        out_shape=jax.ShapeDtypeStruct((T, GDN_V_DIM), bf16),
        scratch_shapes=[pltpu.VMEM((tt + SUBLANES, qw), f32),
                        pltpu.VMEM((tt + SUBLANES, qw), f32),
                        pltpu.VMEM((tt + SUBLANES, vw), f32),
                        pltpu.VMEM((2 * G, hd, hd), f32)],
        compiler_params=_params("arbitrary", "arbitrary", "arbitrary"),
        name="gdn_core",
    )(pm, pm, pm, pm, conv_w, conv_w, conv_w, gc, beta, norm_w)


def _router_body(h_ref, nw_ref, sc_ref, sh_ref, wr_ref, br_ref,
                 u_ref, ids_ref, wts_ref, rk_ref, cnt_ref, *, tm):
    @pl.when(pl.program_id(0) == 0)
    def _():
        cnt_ref[...] = jnp.zeros_like(cnt_ref)

    u = _modulated_norm(h_ref[...], nw_ref[...], sc_ref[0], sh_ref[0])
    u_ref[...] = u
    logits = lax.dot_general(wr_ref[...], u, (((1,), (1,)), ((), ())),
                             precision=lax.Precision.HIGHEST,
                             preferred_element_type=f32) + br_ref[...]
    eidx = lax.broadcasted_iota(i32, logits.shape, 0)
    cur = logits
    vals, sels = [], []
    for _ in range(TOP_K):
        m = jnp.max(cur, axis=0, keepdims=True)
        sel = jnp.min(jnp.where(cur == m, eidx, N_EXPERTS), axis=0, keepdims=True)
        vals.append(m)
        sels.append(sel)
        cur = jnp.where(eidx == sel, -jnp.inf, cur)
    ex = [jnp.exp(v - vals[0]) for v in vals]
    den = ex[0] + ex[1] + ex[2] + ex[3]
    wts_ref[...] = jnp.concatenate([e / den for e in ex], axis=0)
    ids_ref[...] = jnp.concatenate(sels, axis=0)
    onehots = [eidx == s for s in sels]
    chosen = jnp.zeros(logits.shape, f32)
    for oh in onehots:
        chosen = chosen + jnp.where(oh, 1.0, 0.0)
    ti = lax.broadcasted_iota(i32, (tm, tm), 0)
    tj = lax.broadcasted_iota(i32, (tm, tm), 1)
    before = jnp.where(ti < tj, 1.0, 0.0)
    base = _dot(chosen, before) + cnt_ref[:, 0:1]
    ranks = [jnp.sum(jnp.where(oh, base, 0.0), axis=0, keepdims=True) for oh in onehots]
    rk_ref[...] = jnp.concatenate(ranks, axis=0).astype(i32)
    cnt_ref[...] = cnt_ref[...] + jnp.sum(chosen, axis=1, keepdims=True)


def _router(h, nw, scale, shift, w_r_t, b_r, S, tm):
    T, D = h.shape
    tpb = S // tm
    lanes = 128
    return pl.pallas_call(
        functools.partial(_router_body, tm=tm),
        grid=(T // tm,),
        in_specs=[pl.BlockSpec((tm, D), lambda i: (i, 0)),
                  pl.BlockSpec((1, D), lambda i: (0, 0)),
                  pl.BlockSpec((1, 1, D), lambda i: (i // tpb, 0, 0)),
                  pl.BlockSpec((1, 1, D), lambda i: (i // tpb, 0, 0)),
                  pl.BlockSpec((N_EXPERTS, D), lambda i: (0, 0)),
                  pl.BlockSpec((N_EXPERTS, 1), lambda i: (0, 0))],
        out_specs=[pl.BlockSpec((tm, D), lambda i: (i, 0)),
                   pl.BlockSpec((TOP_K, tm), lambda i: (0, i)),
                   pl.BlockSpec((TOP_K, tm), lambda i: (0, i)),
                   pl.BlockSpec((TOP_K, tm), lambda i: (0, i)),
                   pl.BlockSpec((N_EXPERTS, lanes), lambda i: (0, 0))],
        out_shape=[jax.ShapeDtypeStruct((T, D), f32),
                   jax.ShapeDtypeStruct((TOP_K, T), i32),
                   jax.ShapeDtypeStruct((TOP_K, T), f32),
                   jax.ShapeDtypeStruct((TOP_K, T), i32),
                   jax.ShapeDtypeStruct((N_EXPERTS, lanes), f32)],
        compiler_params=_params("arbitrary"),
        name="moe_router",
    )(h, nw, scale, shift, w_r_t, b_r)


def _gather_pipeline(idx_hbm, src_hbm, idx_smem, buf, isem, gsem, n_idx):
    i = pl.program_id(0)
    nblk = pl.num_programs(0)
    slot = lax.rem(i, 2)
    nslot = 1 - slot

    def idx_copy(blk, s):
        return pltpu.make_async_copy(idx_hbm.at[blk], idx_smem.at[s], isem.at[s])

    def row_copy(s, r, row):
        return pltpu.make_async_copy(src_hbm.at[pl.ds(row, 1), :], buf.at[s, pl.ds(r, 1), :], gsem.at[s])

    def issue_rows(s):
        def body(r, carry):
            row_copy(s, r, idx_smem[s, 0, r]).start()
            return carry
        lax.fori_loop(0, n_idx, body, 0, unroll=8)

    def wait_rows(s):
        def body(r, carry):
            row_copy(s, r, 0).wait()
            return carry
        lax.fori_loop(0, n_idx, body, 0, unroll=8)

    @pl.when(i == 0)
    def _():
        first = idx_copy(0, 0)
        first.start()
        first.wait()
        issue_rows(0)

        @pl.when(nblk > 1)
        def _():
            idx_copy(1, 1).start()

    @pl.when(i + 1 < nblk)
    def _():
        idx_copy(i + 1, nslot).wait()
        issue_rows(nslot)

    @pl.when(i + 2 < nblk)
    def _():
        idx_copy(i + 2, slot).start()

    wait_rows(slot)
    return slot


def _moe_body(be_ref, nu_ref, rt_hbm, u_hbm, wgu_ref, bgu_ref, wd_ref, bd_ref, y_ref,
              idx_smem, xbuf, isem, gsem):
    slot = _gather_pipeline(rt_hbm, u_hbm, idx_smem, xbuf, isem, gsem, MOE_ROWS)
    used = pl.program_id(0) < nu_ref[0]

    @pl.when(used)
    def _():
        dff = wd_ref.shape[1]
        gu = _dot(xbuf[slot], wgu_ref[0]) + bgu_ref[0]
        glu = jnp.minimum(gu[:, :dff], SWIGLU_LIMIT)
        lin = jnp.clip(gu[:, dff:], -SWIGLU_LIMIT, SWIGLU_LIMIT)
        hid = glu * jax.nn.sigmoid(SWIGLU_ALPHA * glu) * (lin + 1.0)
        y_ref[...] = _dot(hid, wd_ref[0]) + bd_ref[0]

    @pl.when(jnp.logical_not(used))
    def _():
        y_ref[...] = jnp.zeros_like(y_ref)


def _moe_experts(blk_expert, n_used, row_tok, u, w_gu, b_gu, w_down, b_down):
    nblk = row_tok.shape[0]
    T, D = u.shape
    dff2 = w_gu.shape[2]
    dff = w_down.shape[1]
    grid_spec = pltpu.PrefetchScalarGridSpec(
        num_scalar_prefetch=2,
        grid=(nblk,),
        in_specs=[pl.BlockSpec(memory_space=pl.ANY),
                  pl.BlockSpec(memory_space=pl.ANY),
                  pl.BlockSpec((1, D, dff2), lambda i, be, nu: (be[i], 0, 0)),
                  pl.BlockSpec((1, 1, dff2), lambda i, be, nu: (be[i], 0, 0)),
                  pl.BlockSpec((1, dff, D), lambda i, be, nu: (be[i], 0, 0)),
                  pl.BlockSpec((1, 1, D), lambda i, be, nu: (be[i], 0, 0))],
        out_specs=pl.BlockSpec((MOE_ROWS, D), lambda i, be, nu: (i, 0)),
        scratch_shapes=[pltpu.SMEM((2, 1, MOE_ROWS), i32),
                        pltpu.VMEM((2, MOE_ROWS, D), f32),
                        pltpu.SemaphoreType.DMA((2,)),
                        pltpu.SemaphoreType.DMA((2,))],
    )
    return pl.pallas_call(
        _moe_body,
        grid_spec=grid_spec,
        out_shape=jax.ShapeDtypeStruct((nblk * MOE_ROWS, D), f32),
        compiler_params=_params("arbitrary"),
        name="moe_experts",
    )(blk_expert, n_used, row_tok, u, w_gu, b_gu, w_down, b_down)


def _combine_body(d_hbm, y_hbm, h_ref, w_ref, g_ref, o_ref, idx_smem, ybuf, isem, gsem, *, tm):
    slot = _gather_pipeline(d_hbm, y_hbm, idx_smem, ybuf, isem, gsem, TOP_K * tm)
    w = w_ref[...]
    mix = w[:, 0:1] * ybuf[slot, 0:tm, :]
    for k in range(1, TOP_K):
        mix = mix + w[:, k:k + 1] * ybuf[slot, k * tm:(k + 1) * tm, :]
    o_ref[...] = h_ref[...] + g_ref[0] * mix


def _moe_combine(dest_blk, y, h, wts_t, gate, S, tm):
    T, D = h.shape
    tpb = S // tm
    return pl.pallas_call(
        functools.partial(_combine_body, tm=tm),
        grid=(T // tm,),
        in_specs=[pl.BlockSpec(memory_space=pl.ANY),
                  pl.BlockSpec(memory_space=pl.ANY),
                  pl.BlockSpec((tm, D), lambda i: (i, 0)),
                  pl.BlockSpec((tm, TOP_K), lambda i: (i, 0)),
                  pl.BlockSpec((1, 1, D), lambda i: (i // tpb, 0, 0))],
        out_specs=pl.BlockSpec((tm, D), lambda i: (i, 0)),
        out_shape=jax.ShapeDtypeStruct((T, D), f32),
        scratch_shapes=[pltpu.SMEM((2, 1, TOP_K * tm), i32),
                        pltpu.VMEM((2, TOP_K * tm, D), f32),
                        pltpu.SemaphoreType.DMA((2,)),
                        pltpu.SemaphoreType.DMA((2,))],
        compiler_params=_params("arbitrary"),
        name="moe_combine",
    )(dest_blk, y, h, wts_t, gate)


def _moe_layer(h, nw, scale, shift, gate, w_router, b_router, w_gu, b_gu, w_down, b_down, S):
    T, D = h.shape
    tm_r = min(512, S)
    u, ids, wts, ranks, cnt = _router(h, nw, scale, shift, w_router.T, b_router.reshape(N_EXPERTS, 1),
                                      S, tm_r)
    counts = cnt[:, 0].astype(i32)
    padded = (counts + MOE_ROWS - 1) // MOE_ROWS * MOE_ROWS
    pad_end = jnp.cumsum(padded)
    pad_start = pad_end - padded
    eids = jnp.arange(N_EXPERTS, dtype=i32)[:, None, None]
    dest = jnp.sum(jnp.where(ids[None] == eids, pad_start[:, None, None], 0), axis=0) + ranks
    nblk = -(-(T * TOP_K + N_EXPERTS * (MOE_ROWS - 1)) // MOE_ROWS)
    tok = jnp.broadcast_to(jnp.arange(T, dtype=i32)[None, :], (TOP_K, T))
    row_tok = jnp.zeros((nblk * MOE_ROWS,), i32).at[dest.reshape(-1)].set(tok.reshape(-1))
    blk_start = jnp.arange(nblk, dtype=i32) * MOE_ROWS
    blk_expert = jnp.minimum(jnp.searchsorted(pad_end, blk_start, side="right"),
                             N_EXPERTS - 1).astype(i32)
    n_used = (pad_end[-1:] // MOE_ROWS).astype(i32)
    y = _moe_experts(blk_expert, n_used, row_tok.reshape(nblk, 1, MOE_ROWS), u,
                     w_gu.astype(bf16), b_gu.reshape(N_EXPERTS, 1, -1),
                     w_down.astype(bf16), b_down.reshape(N_EXPERTS, 1, -1))
    tm_c = min(256, S)
    dest_blk = dest.reshape(TOP_K, T // tm_c, tm_c).transpose(1, 0, 2).reshape(T // tm_c, 1, TOP_K * tm_c)
    return _moe_combine(dest_blk, y, h, wts.T, gate, S, tm_c)


def _final_norm_body(h_ref, nw_ref, o_ref):
    x = h_ref[...]
    ms = jnp.mean(x * x, axis=-1, keepdims=True)
    o_ref[...] = x * lax.rsqrt(ms + RMS_EPS) * nw_ref[...]


def _final_norm(h, nw, tm):
    T, D = h.shape
    return pl.pallas_call(
        _final_norm_body,
        grid=(T // tm,),
        in_specs=[pl.BlockSpec((tm, D), lambda i: (i, 0)),
                  pl.BlockSpec((1, D), lambda i: (0, 0))],
        out_specs=pl.BlockSpec((tm, D), lambda i: (i, 0)),
        out_shape=jax.ShapeDtypeStruct((T, D), f32),
        compiler_params=_params("arbitrary"),
        name="final_norm",
    )(h, nw)


def kernel(x, c, w_ada, ada_table, norm_mix, norm_ffn, norm_final, lru_w_in, lru_conv_w, lru_conv_b, lru_w_rg, lru_b_rg, lru_w_ig, lru_b_ig, lru_lambda, lru_w_out, gdn_w_in, gdn_conv_w, gdn_a_log, gdn_dt_bias, gdn_norm, gdn_w_out, moe_w_router, moe_b_router, moe_w_gate_up, moe_b_gate_up, moe_w_down, moe_b_down):
    B, S, D = x.shape
    T = B * S
    depth = ada_table.shape[0]
    tm = min(1024, S)
    tt_lru = min(256, S)
    tt_gdn = min(256, S)

    cond = _ada(c, w_ada).reshape(B, N_MODS, D)
    h = x.reshape(T, D)
    for layer in range(depth):
        mods = cond + ada_table[layer]
        shift_m, scale_m, gate_m, shift_f, scale_f, gate_f = [mods[:, j:j + 1, :] for j in range(N_MODS)]
        nw = norm_mix[layer].reshape(1, D)
        j = layer // 2
        if layer % 2 == 0:
            proj = _norm_matmul(h, nw, scale_m, shift_m, lru_w_in[j].astype(bf16), None, S, tm, 1024)
            w_gates = jnp.concatenate([lru_w_rg[j], lru_w_ig[j]], axis=-1).astype(bf16)
            act = _lru_core(proj, lru_conv_w[j], lru_conv_b[j].reshape(1, D), w_gates,
                            lru_b_rg[j].reshape(1, D), lru_b_ig[j].reshape(1, D),
                            lru_lambda[j].reshape(1, D), B, S, tt_lru)
            h = _matmul_residual(act, lru_w_out[j].astype(bf16), h, gate_m, S, tm, 512)
        else:
            w_in = gdn_w_in[j]
            n_main = GDN_CONV_DIM + GDN_V_DIM
            pm, ba = _norm_matmul(h, nw, scale_m, shift_m, w_in[:, :n_main].astype(bf16),
                                  w_in[:, n_main:].astype(bf16), S, tm, 1024)
            beta, gc = _gdn_gates(ba, gdn_a_log[j].reshape(1, -1), gdn_dt_bias[j].reshape(1, -1),
                                  min(512, S))
            act = _gdn_core(pm, gdn_conv_w[j], gc, beta, gdn_norm[j].reshape(1, -1), B, S, tt_gdn, 4)
            h = _matmul_residual(act, gdn_w_out[j].astype(bf16), h, gate_m, S, tm, 512)
        h = _moe_layer(h, norm_ffn[layer].reshape(1, D), scale_f, shift_f, gate_f,
                       moe_w_router[layer], moe_b_router[layer], moe_w_gate_up[layer],
                       moe_b_gate_up[layer], moe_w_down[layer], moe_b_down[layer], S)
    return _final_norm(h, norm_final.reshape(1, D), tm).reshape(B, S, D)
```

```python
import functools

import jax
import jax.numpy as jnp
from jax import lax
from jax.experimental import pallas as pl
from jax.experimental.pallas import tpu as pltpu

f32 = jnp.float32
bf16 = jnp.bfloat16
i32 = jnp.int32

N_MODS = 6
CONV_WIDTH = 4
RMS_EPS = 1e-6
L2_EPS = 1e-6

LRU_HEADS = 16
LRU_HEAD_DIM = 128
LRU_C = 8.0

GDN_HEAD_DIM = 128
GDN_KEY_HEADS = 16
GDN_VALUE_HEADS = 32
GDN_QK_DIM = GDN_KEY_HEADS * GDN_HEAD_DIM
GDN_V_DIM = GDN_VALUE_HEADS * GDN_HEAD_DIM
GDN_CONV_DIM = 2 * GDN_QK_DIM + GDN_V_DIM
CHUNK = 64

N_EXPERTS = 32
TOP_K = 4
SWIGLU_ALPHA = 1.702
SWIGLU_LIMIT = 7.0
MOE_ROWS = 256

VMEM_LIMIT_BYTES = 56 * 1024 * 1024
SUBLANES = 8


def _params(*sem):
    return pltpu.CompilerParams(dimension_semantics=sem, vmem_limit_bytes=VMEM_LIMIT_BYTES)


def _dot(a, b):
    return jnp.dot(a.astype(bf16), b.astype(bf16), preferred_element_type=f32)


def _dot_nt(a, b):
    return lax.dot_general(a.astype(bf16), b.astype(bf16), (((1,), (1,)), ((), ())),
                           preferred_element_type=f32)


def _dot_tn(a, b):
    return lax.dot_general(a.astype(bf16), b.astype(bf16), (((0,), (0,)), ((), ())),
                           preferred_element_type=f32)


def _silu(x):
    return x * jax.nn.sigmoid(x)


def _softplus(x):
    return jnp.maximum(x, 0.0) + jnp.log(1.0 + jnp.exp(-jnp.abs(x)))


def _modulated_norm(x, nw, scale, shift):
    ms = jnp.mean(x * x, axis=-1, keepdims=True)
    y = x * lax.rsqrt(ms + RMS_EPS) * nw
    return y * (1.0 + scale) + shift


def _ada_body(c_ref, w_ref, o_ref):
    o_ref[...] = _dot(_silu(c_ref[...]), w_ref[...])


def _ada(c, w_ada):
    B, D = c.shape
    N = w_ada.shape[1]
    tn = 1024
    return pl.pallas_call(
        _ada_body,
        grid=(N // tn,),
        in_specs=[pl.BlockSpec((B, D), lambda j: (0, 0)),
                  pl.BlockSpec((D, tn), lambda j: (0, j))],
        out_specs=pl.BlockSpec((B, tn), lambda j: (0, j)),
        out_shape=jax.ShapeDtypeStruct((B, N), f32),
        compiler_params=_params("arbitrary"),
        name="ada",
    )(c, w_ada)


def _norm_mm_body(h_ref, nw_ref, sc_ref, sh_ref, w_ref, *rest, has_small):
    if has_small:
        ws_ref, o_ref, os_ref, u_ref = rest
    else:
        o_ref, u_ref = rest

    @pl.when(pl.program_id(1) == 0)
    def _():
        u = _modulated_norm(h_ref[...], nw_ref[...], sc_ref[0], sh_ref[0]).astype(bf16)
        u_ref[...] = u
        if has_small:
            os_ref[...] = jnp.dot(u, ws_ref[...], preferred_element_type=f32)

    o_ref[...] = jnp.dot(u_ref[...], w_ref[...], preferred_element_type=f32).astype(o_ref.dtype)


def _norm_matmul(h, nw, scale, shift, w, w_small, S, tm, tn):
    T, D = h.shape
    N = w.shape[1]
    tpb = S // tm
    in_specs = [pl.BlockSpec((tm, D), lambda i, j: (i, 0)),
                pl.BlockSpec((1, D), lambda i, j: (0, 0)),
                pl.BlockSpec((1, 1, D), lambda i, j: (i // tpb, 0, 0)),
                pl.BlockSpec((1, 1, D), lambda i, j: (i // tpb, 0, 0)),
                pl.BlockSpec((D, tn), lambda i, j: (0, j))]
    out_specs = [pl.BlockSpec((tm, tn), lambda i, j: (i, j))]
    out_shape = [jax.ShapeDtypeStruct((T, N), f32)]
    args = [h, nw, scale, shift, w]
    if w_small is not None:
        ns = w_small.shape[1]
        in_specs.append(pl.BlockSpec((D, ns), lambda i, j: (0, 0)))
        out_specs.append(pl.BlockSpec((tm, ns), lambda i, j: (i, 0)))
        out_shape.append(jax.ShapeDtypeStruct((T, ns), f32))
        args.append(w_small)
    res = pl.pallas_call(
        functools.partial(_norm_mm_body, has_small=w_small is not None),
        grid=(T // tm, N // tn),
        in_specs=in_specs, out_specs=out_specs, out_shape=out_shape,
        scratch_shapes=[pltpu.VMEM((tm, D), bf16)],
        compiler_params=_params("arbitrary", "arbitrary"),
        name="norm_matmul",
    )(*args)
    return res if w_small is not None else res[0]


def _mm_res_body(a_ref, w_ref, h_ref, g_ref, o_ref):
    acc = jnp.dot(a_ref[...], w_ref[...], preferred_element_type=f32)
    o_ref[...] = h_ref[...] + g_ref[0] * acc


def _matmul_residual(a, w, h, gate, S, tm, tn):
    T, K = a.shape
    D = w.shape[1]
    tpb = S // tm
    return pl.pallas_call(
        _mm_res_body,
        grid=(T // tm, D // tn),
        in_specs=[pl.BlockSpec((tm, K), lambda i, j: (i, 0)),
                  pl.BlockSpec((K, tn), lambda i, j: (0, j)),
                  pl.BlockSpec((tm, tn), lambda i, j: (i, j)),
                  pl.BlockSpec((1, 1, tn), lambda i, j: (i // tpb, 0, j))],
        out_specs=pl.BlockSpec((tm, tn), lambda i, j: (i, j)),
        out_shape=jax.ShapeDtypeStruct((T, D), f32),
        compiler_params=_params("arbitrary", "arbitrary"),
        name="matmul_residual",
    )(a, w, h, gate)


def _causal_conv(x_ref, pad_ref, cw_ref, tt):
    pad_ref[SUBLANES:SUBLANES + tt, :] = x_ref[...].astype(f32)
    base = SUBLANES - (CONV_WIDTH - 1)
    acc = cw_ref[0:1, :] * pad_ref[pl.ds(base, tt), :]
    for k in range(1, CONV_WIDTH):
        acc = acc + cw_ref[k:k + 1, :] * pad_ref[pl.ds(base + k, tt), :]
    pad_ref[0:SUBLANES, :] = pad_ref[tt:tt + SUBLANES, :]
    return acc


def _lru_body(xb_ref, yb_ref, cw_ref, cb_ref, wg_ref, brg_ref, big_ref, lam_ref, o_ref,
              pad_ref, hstate_ref, a_ref, b_ref, hs_ref, *, tt):
    @pl.when(pl.program_id(1) == 0)
    def _():
        pad_ref[0:SUBLANES, :] = jnp.zeros((SUBLANES, pad_ref.shape[1]), f32)
        hstate_ref[...] = jnp.zeros_like(hstate_ref)

    xc = _causal_conv(xb_ref, pad_ref, cw_ref, tt) + cb_ref[...]
    neg_c_sp = -LRU_C * _softplus(-lam_ref[...])
    hd = LRU_HEAD_DIM
    for hh in range(LRU_HEADS):
        sl = slice(hh * hd, (hh + 1) * hd)
        xh = xc[:, sl]
        gates = _dot(xh, wg_ref[hh])
        r = jax.nn.sigmoid(gates[:, :hd] + brg_ref[:, sl])
        ig = jax.nn.sigmoid(gates[:, hd:] + big_ref[:, sl])
        a = jnp.exp(r * neg_c_sp[:, sl])
        a_ref[:, sl] = a
        b_ref[:, sl] = jnp.sqrt(jnp.maximum(1.0 - a * a, 0.0)) * (ig * xh)

    def step(i, h):
        h = a_ref[pl.ds(i, 1), :] * h + b_ref[pl.ds(i, 1), :]
        hs_ref[pl.ds(i, 1), :] = h
        return h

    hstate_ref[...] = lax.fori_loop(0, tt, step, hstate_ref[...], unroll=8)
    o_ref[...] = (hs_ref[...] * jax.nn.gelu(yb_ref[...].astype(f32), approximate=True)).astype(o_ref.dtype)


def _lru_core(proj, conv_w, conv_b, w_gates, b_rg, b_ig, lam, B, S, tt):
    T = proj.shape[0]
    D = proj.shape[1] // 2
    nt = S // tt
    vec = pl.BlockSpec((1, D), lambda b, t: (0, 0))
    return pl.pallas_call(
        functools.partial(_lru_body, tt=tt),
        grid=(B, nt),
        in_specs=[pl.BlockSpec((tt, D), lambda b, t: (b * nt + t, 0)),
                  pl.BlockSpec((tt, D), lambda b, t: (b * nt + t, 1)),
                  pl.BlockSpec((CONV_WIDTH, D), lambda b, t: (0, 0)),
                  vec,
                  pl.BlockSpec((LRU_HEADS, LRU_HEAD_DIM, 2 * LRU_HEAD_DIM), lambda b, t: (0, 0, 0)),
                  vec, vec, vec],
        out_specs=pl.BlockSpec((tt, D), lambda b, t: (b * nt + t, 0)),
        out_shape=jax.ShapeDtypeStruct((T, D), bf16),
        scratch_shapes=[pltpu.VMEM((tt + SUBLANES, D), f32),
                        pltpu.VMEM((1, D), f32),
                        pltpu.VMEM((tt, D), f32),
                        pltpu.VMEM((tt, D), f32),
                        pltpu.VMEM((tt, D), f32)],
        compiler_params=_params("arbitrary", "arbitrary"),
        name="lru_core",
    )(proj, proj, conv_w, conv_b, w_gates, b_rg, b_ig, lam)


def _gdn_gate_body(ba_ref, alog_ref, dtb_ref, beta_ref, gc_ref, *, tt):
    nh = GDN_VALUE_HEADS
    ba = ba_ref[...]
    beta_ref[...] = jax.nn.sigmoid(ba[:, :nh])
    g = -jnp.exp(alog_ref[...]) * _softplus(ba[:, nh:] + dtb_ref[...])
    ri = lax.broadcasted_iota(i32, (tt, tt), 0)
    ci = lax.broadcasted_iota(i32, (tt, tt), 1)
    same_chunk = (ri // CHUNK) == (ci // CHUNK)
    tri = jnp.where(same_chunk & (ci <= ri), 1.0, 0.0).astype(f32)
    gc_ref[...] = jnp.dot(tri, g, precision=lax.Precision.HIGHEST, preferred_element_type=f32)


def _gdn_gates(ba, a_log, dt_bias, tt):
    T = ba.shape[0]
    nh = GDN_VALUE_HEADS
    out = jax.ShapeDtypeStruct((T, nh), f32)
    return pl.pallas_call(
        functools.partial(_gdn_gate_body, tt=tt),
        grid=(T // tt,),
        in_specs=[pl.BlockSpec((tt, 2 * nh), lambda i: (i, 0)),
                  pl.BlockSpec((1, nh), lambda i: (0, 0)),
                  pl.BlockSpec((1, nh), lambda i: (0, 0))],
        out_specs=[pl.BlockSpec((tt, nh), lambda i: (i, 0)),
                   pl.BlockSpec((tt, nh), lambda i: (i, 0))],
        out_shape=[out, out],
        compiler_params=_params("arbitrary"),
        name="gdn_gates",
    )(ba, a_log, dt_bias)


def _gdn_body(q_ref, k_ref, v_ref, z_ref, cwq_ref, cwk_ref, cwv_ref, gc_ref, be_ref, nw_ref, o_ref,
              qpad_ref, kpad_ref, vpad_ref, state_ref, *, G, tt):
    hd = GDN_HEAD_DIM
    group = GDN_VALUE_HEADS // GDN_KEY_HEADS

    @pl.when(pl.program_id(2) == 0)
    def _():
        for p in (qpad_ref, kpad_ref, vpad_ref):
            p[0:SUBLANES, :] = jnp.zeros((SUBLANES, p.shape[1]), f32)
        state_ref[...] = jnp.zeros_like(state_ref)

    q = _silu(_causal_conv(q_ref, qpad_ref, cwq_ref, tt))
    k = _silu(_causal_conv(k_ref, kpad_ref, cwk_ref, tt))
    v = _silu(_causal_conv(v_ref, vpad_ref, cwv_ref, tt))
    gcb = gc_ref[...]
    beb = be_ref[...]
    nchunk = tt // CHUNK
    heads = G * group
    pw_rows = group * CHUNK
    assert pw_rows == hd
    ri = lax.broadcasted_iota(i32, (pw_rows, pw_rows), 0)
    ci = lax.broadcasted_iota(i32, (pw_rows, pw_rows), 1)
    same_head = (ri // CHUNK) == (ci // CHUNK)
    causal = same_head & (ri >= ci)
    strict = same_head & (ri > ci)
    eye = ri == ci
    col_head = lax.broadcasted_iota(i32, (1, pw_rows), 1) // CHUNK

    ones_hd = jnp.ones((hd, hd), bf16)

    def row_sumsq(x):
        return jnp.dot((x * x).astype(bf16), ones_hd, preferred_element_type=f32)

    qn, kn = [], []
    for kh in range(G):
        qh = q[:, kh * hd:(kh + 1) * hd]
        kk = k[:, kh * hd:(kh + 1) * hd]
        qn.append(qh * (lax.rsqrt(row_sumsq(qh) + L2_EPS) * (hd ** -0.5)))
        kn.append(kk * lax.rsqrt(row_sumsq(kk) + L2_EPS))
    pick_shape = (GDN_VALUE_HEADS, heads * hd)
    picked_head = pl.program_id(1) * heads + lax.broadcasted_iota(i32, pick_shape, 1) // hd
    pick = jnp.where(lax.broadcasted_iota(i32, pick_shape, 0) == picked_head, 1.0, 0.0).astype(bf16)
    g_hi = gcb.astype(bf16)
    g_lo = (gcb - g_hi.astype(f32)).astype(bf16)
    g_rep = (jnp.dot(g_hi, pick, preferred_element_type=f32)
             + jnp.dot(g_lo, pick, preferred_element_type=f32))
    b_rep = jnp.dot(beb.astype(bf16), pick, preferred_element_type=f32)
    g_all = [g_rep[:, hl * hd:(hl + 1) * hd] for hl in range(heads)]
    b_all = [b_rep[:, hl * hd:(hl + 1) * hd] for hl in range(heads)]

    def stack_heads(per_head):
        return jnp.concatenate(per_head, axis=0)

    items = [(kh, c) for c in range(nchunk) for kh in range(G)]
    rows_of = lambda c: slice(c * CHUNK, (c + 1) * CHUNK)
    decay, a_mat, xs, q_dec, k_dec_t, s_gain, qk = {}, {}, {}, {}, {}, {}, {}
    for p in items:
        kh, c = p
        hs = [kh * group + j for j in range(group)]
        kc = kn[kh][rows_of(c)]
        k2 = stack_heads([kc] * group)
        q2 = stack_heads([qn[kh][rows_of(c)]] * group)
        kt = kc.T
        kt2 = jnp.concatenate([kt] * group, axis=1)
        gcol = stack_heads([g_all[h][rows_of(c)] for h in hs])
        bcol = stack_heads([b_all[h][rows_of(c)] for h in hs])
        grow = jnp.sum(jnp.where(eye, gcol, 0.0), axis=0, keepdims=True)
        decay[p] = jnp.where(causal, jnp.exp(gcol - grow), 0.0)
        a_mat[p] = jnp.where(strict, bcol * _dot(k2, kt2) * decay[p], 0.0)
        qk[p] = _dot(q2, kt2) * decay[p]
        eg = jnp.exp(gcol)
        q_dec[p] = q2 * eg
        v2 = stack_heads([v[:, h * hd:(h + 1) * hd][rows_of(c)] for h in hs])
        xs[p] = jnp.concatenate([v2 * bcol, k2 * (bcol * eg)], axis=1)
        lasts = [g_all[h][c * CHUNK + CHUNK - 1:(c + 1) * CHUNK, :] for h in hs]
        gl_row = lasts[0]
        for j in range(1, group):
            gl_row = jnp.where(col_head == j, lasts[j], gl_row)
        for j in range(group):
            s_gain[kh * group + j, c] = jnp.exp(lasts[j])
        kd = kt2 * jnp.exp(gl_row - grow)
        k_dec_t[p] = stack_heads([jnp.where(col_head == j, kd, 0.0) for j in range(group)])
    for p in items:
        xs[p] = xs[p] - _dot(a_mat[p], xs[p])
    pw = dict(a_mat)
    span = 2
    while span < CHUNK:
        for p in items:
            pw[p] = _dot(pw[p], pw[p])
        for p in items:
            xs[p] = xs[p] + _dot(pw[p], xs[p])
        span *= 2
    lhs, o_add, s_add = {}, {}, {}
    for p in items:
        kh, c = p
        pq = _dot(qk[p], xs[p])
        mk = _dot(k_dec_t[p], xs[p])
        q_eff = q_dec[p] - pq[:, hd:]
        for j in range(group):
            n = (kh * group + j, c)
            o_add[n] = pq[j * CHUNK:(j + 1) * CHUNK, :hd]
            s_add[n] = mk[j * hd:(j + 1) * hd, :hd]
            lhs[n] = jnp.concatenate([q_eff[j * CHUNK:(j + 1) * CHUNK], mk[j * hd:(j + 1) * hd, hd:]], axis=0)
    st = [state_ref[hl] for hl in range(heads)]
    outs = [[] for _ in range(heads)]
    for c in range(nchunk):
        for hl in range(heads):
            n = (hl, c)
            r = _dot(lhs[n], st[hl])
            outs[hl].append(r[:CHUNK] + o_add[n])
            st[hl] = st[hl] * s_gain[n] - r[CHUNK:] + s_add[n]
    for hl in range(heads):
        state_ref[hl] = st[hl]
        o_all = jnp.concatenate(outs[hl], axis=0)
        zz = z_ref[:, hl * hd:(hl + 1) * hd].astype(f32)
        on = o_all * lax.rsqrt(row_sumsq(o_all) * (1.0 / hd) + RMS_EPS) * nw_ref[...]
        o_ref[:, hl * hd:(hl + 1) * hd] = (on * _silu(zz)).astype(o_ref.dtype)


def _gdn_core(pm, conv_w, gc, beta, norm_w, B, S, tt, G):
    T = pm.shape[0]
    hd = GDN_HEAD_DIM
    nt = S // tt
    qw = G * hd
    vw = 2 * G * hd
    nq = GDN_QK_DIM // qw
    nv = GDN_V_DIM // vw
    row = lambda b, g, t: b * nt + t
    return pl.pallas_call(
        functools.partial(_gdn_body, G=G, tt=tt),
        grid=(B, GDN_KEY_HEADS // G, nt),
        in_specs=[pl.BlockSpec((tt, qw), lambda b, g, t: (row(b, g, t), g)),
                  pl.BlockSpec((tt, qw), lambda b, g, t: (row(b, g, t), nq + g)),
                  pl.BlockSpec((tt, vw), lambda b, g, t: (row(b, g, t), nv + g)),
                  pl.BlockSpec((tt, vw), lambda b, g, t: (row(b, g, t), 2 * nv + g)),
                  pl.BlockSpec((CONV_WIDTH, qw), lambda b, g, t: (0, g)),
                  pl.BlockSpec((CONV_WIDTH, qw), lambda b, g, t: (0, nq + g)),
                  pl.BlockSpec((CONV_WIDTH, vw), lambda b, g, t: (0, nv + g)),
                  pl.BlockSpec((tt, GDN_VALUE_HEADS), lambda b, g, t: (row(b, g, t), 0)),
                  pl.BlockSpec((tt, GDN_VALUE_HEADS), lambda b, g, t: (row(b, g, t), 0)),
                  pl.BlockSpec((1, hd), lambda b, g, t: (0, 0))],
        out_specs=pl.BlockSpec((tt, vw), lambda b, g, t: (row(b, g, t), g)),
        out_shape=jax.ShapeDtypeStruct((T, GDN_V_DIM), bf16),
        scratch_shapes=[pltpu.VMEM((tt + SUBLANES, qw), f32),
                        pltpu.VMEM((tt + SUBLANES, qw), f32),
                        pltpu.VMEM((tt + SUBLANES, vw), f32),
                        pltpu.VMEM((2 * G, hd, hd), f32)],
        compiler_params=_params("arbitrary", "arbitrary", "arbitrary"),
        name="gdn_core",
    )(pm, pm, pm, pm, conv_w, conv_w, conv_w, gc, beta, norm_w)


def _router_body(h_ref, nw_ref, sc_ref, sh_ref, wr_ref, br_ref,
                 u_ref, ids_ref, wts_ref, rk_ref, cnt_ref, *, tm):
    @pl.when(pl.program_id(0) == 0)
    def _():
        cnt_ref[...] = jnp.zeros_like(cnt_ref)

    u = _modulated_norm(h_ref[...], nw_ref[...], sc_ref[0], sh_ref[0])
    u_ref[...] = u
    logits = lax.dot_general(wr_ref[...], u, (((1,), (1,)), ((), ())),
                             precision=lax.Precision.HIGHEST,
                             preferred_element_type=f32) + br_ref[...]
    eidx = lax.broadcasted_iota(i32, logits.shape, 0)
    cur = logits
    vals, sels = [], []
    for _ in range(TOP_K):
        m = jnp.max(cur, axis=0, keepdims=True)
        sel = jnp.min(jnp.where(cur == m, eidx, N_EXPERTS), axis=0, keepdims=True)
        vals.append(m)
        sels.append(sel)
        cur = jnp.where(eidx == sel, -jnp.inf, cur)
    ex = [jnp.exp(v - vals[0]) for v in vals]
    den = ex[0] + ex[1] + ex[2] + ex[3]
    wts_ref[...] = jnp.concatenate([e / den for e in ex], axis=0)
    ids_ref[...] = jnp.concatenate(sels, axis=0)
    onehots = [eidx == s for s in sels]
    chosen = jnp.zeros(logits.shape, f32)
    for oh in onehots:
        chosen = chosen + jnp.where(oh, 1.0, 0.0)
    ti = lax.broadcasted_iota(i32, (tm, tm), 0)
    tj = lax.broadcasted_iota(i32, (tm, tm), 1)
    before = jnp.where(ti < tj, 1.0, 0.0)
    base = _dot(chosen, before) + cnt_ref[:, 0:1]
    ranks = [jnp.sum(jnp.where(oh, base, 0.0), axis=0, keepdims=True) for oh in onehots]
    rk_ref[...] = jnp.concatenate(ranks, axis=0).astype(i32)
    cnt_ref[...] = cnt_ref[...] + jnp.sum(chosen, axis=1, keepdims=True)


def _router(h, nw, scale, shift, w_r_t, b_r, S, tm):
    T, D = h.shape
    tpb = S // tm
    lanes = 128
    return pl.pallas_call(
        functools.partial(_router_body, tm=tm),
        grid=(T // tm,),
        in_specs=[pl.BlockSpec((tm, D), lambda i: (i, 0)),
                  pl.BlockSpec((1, D), lambda i: (0, 0)),
                  pl.BlockSpec((1, 1, D), lambda i: (i // tpb, 0, 0)),
                  pl.BlockSpec((1, 1, D), lambda i: (i // tpb, 0, 0)),
                  pl.BlockSpec((N_EXPERTS, D), lambda i: (0, 0)),
                  pl.BlockSpec((N_EXPERTS, 1), lambda i: (0, 0))],
        out_specs=[pl.BlockSpec((tm, D), lambda i: (i, 0)),
                   pl.BlockSpec((TOP_K, tm), lambda i: (0, i)),
                   pl.BlockSpec((TOP_K, tm), lambda i: (0, i)),
                   pl.BlockSpec((TOP_K, tm), lambda i: (0, i)),
                   pl.BlockSpec((N_EXPERTS, lanes), lambda i: (0, 0))],
        out_shape=[jax.ShapeDtypeStruct((T, D), f32),
                   jax.ShapeDtypeStruct((TOP_K, T), i32),
                   jax.ShapeDtypeStruct((TOP_K, T), f32),
                   jax.ShapeDtypeStruct((TOP_K, T), i32),
                   jax.ShapeDtypeStruct((N_EXPERTS, lanes), f32)],
        compiler_params=_params("arbitrary"),
        name="moe_router",
    )(h, nw, scale, shift, w_r_t, b_r)


ISSUE_GROUP = 16


class _RowGather:
    def __init__(self, idx_hbm, src_hbm, idx_smem, buf, isem, gsem, n_idx):
        self.idx_hbm, self.src_hbm, self.idx_smem, self.buf = idx_hbm, src_hbm, idx_smem, buf
        self.isem, self.gsem, self.n_idx = isem, gsem, n_idx

    def idx_copy(self, blk, s):
        return pltpu.make_async_copy(self.idx_hbm.at[blk], self.idx_smem.at[s], self.isem.at[s])

    def row_copy(self, s, r, row):
        return pltpu.make_async_copy(self.src_hbm.at[pl.ds(row, 1), :],
                                     self.buf.at[s, pl.ds(r, 1), :], self.gsem.at[s])

    def issue_range(self, s, r0, n):
        rows = [self.idx_smem[s, 0, r0 + j] for j in range(n)]
        for j in range(n):
            self.row_copy(s, r0 + j, rows[j]).start()

    def issue_rows(self, s):
        def body(g, carry):
            self.issue_range(s, pl.multiple_of(g * ISSUE_GROUP, ISSUE_GROUP), ISSUE_GROUP)
            return carry
        lax.fori_loop(0, self.n_idx // ISSUE_GROUP, body, 0)

    def issue_rows_unrolled(self, s):
        for g in range(self.n_idx // ISSUE_GROUP):
            self.issue_range(s, g * ISSUE_GROUP, ISSUE_GROUP)

    def wait_rows(self, s):
        def body(r, carry):
            self.row_copy(s, r, 0).wait()
            return carry
        lax.fori_loop(0, self.n_idx, body, 0, unroll=8)


def _gather_pipeline(gather):
    i = pl.program_id(0)
    nblk = pl.num_programs(0)
    slot = lax.rem(i, 2)
    nslot = 1 - slot

    @pl.when(i == 0)
    def _():
        first = gather.idx_copy(0, 0)
        first.start()
        first.wait()
        gather.issue_rows(0)

        @pl.when(nblk > 1)
        def _():
            gather.idx_copy(1, 1).start()

    for par in (0, 1):
        @pl.when(jnp.logical_and(i + 1 < nblk, slot == par))
        def _(par=par):
            gather.idx_copy(i + 1, 1 - par).wait()
            gather.issue_rows_unrolled(1 - par)

    @pl.when(i + 2 < nblk)
    def _():
        gather.idx_copy(i + 2, slot).start()

    gather.wait_rows(slot)
    return slot


MOE_NTILE = 256


def _moe_body(be_ref, nu_ref, rt_hbm, u_hbm, wgu_ref, bgu_ref, wd_ref, bd_ref, y_ref,
              idx_smem, xbuf, isem, gsem):
    gather = _RowGather(rt_hbm, u_hbm, idx_smem, xbuf, isem, gsem, MOE_ROWS)
    i = pl.program_id(0)
    n_used = nu_ref[0]
    used = i < n_used
    slot = lax.rem(i, 2)
    nslot = 1 - slot

    @pl.when(i == 0)
    def _():
        first = gather.idx_copy(0, 0)
        first.start()
        first.wait()
        gather.issue_rows(0)
        gather.idx_copy(1, 1).start()

    @pl.when(i <= n_used)
    def _():
        gather.wait_rows(slot)

    def multiply(slot, nslot):
        gather.idx_copy(i + 1, nslot).wait()
        dff = wd_ref.shape[1]
        d_out = wd_ref.shape[2]
        n_up = 2 * dff // MOE_NTILE
        n_down = d_out // MOE_NTILE
        n_chunks = n_up + n_down
        bounds = [(MOE_ROWS * c) // n_chunks for c in range(n_chunks + 1)]

        def issue_chunk(c):
            gather.issue_range(nslot, bounds[c], bounds[c + 1] - bounds[c])

        x = xbuf[slot].astype(bf16)
        gu = []
        for j in range(n_up):
            issue_chunk(j)
            cols = slice(j * MOE_NTILE, (j + 1) * MOE_NTILE)
            gu.append(jnp.dot(x, wgu_ref[0, :, cols], preferred_element_type=f32) + bgu_ref[0, :, cols])
        gu = jnp.concatenate(gu, axis=1)
        glu = jnp.minimum(gu[:, :dff], SWIGLU_LIMIT)
        lin = jnp.clip(gu[:, dff:], -SWIGLU_LIMIT, SWIGLU_LIMIT)
        hid = (glu * jax.nn.sigmoid(SWIGLU_ALPHA * glu) * (lin + 1.0)).astype(bf16)
        for j in range(n_down):
            issue_chunk(n_up + j)
            cols = slice(j * MOE_NTILE, (j + 1) * MOE_NTILE)
            y_ref[:, cols] = jnp.dot(hid, wd_ref[0, :, cols], preferred_element_type=f32) + bd_ref[0, :, cols]

    for par in (0, 1):
        @pl.when(jnp.logical_and(used, slot == par))
        def _(par=par):
            multiply(par, 1 - par)

    @pl.when(i + 1 < n_used)
    def _():
        gather.idx_copy(i + 2, slot).start()

    @pl.when(jnp.logical_not(used))
    def _():
        y_ref[...] = jnp.zeros_like(y_ref)


def _moe_experts(blk_expert, n_used, row_tok, u, w_gu, b_gu, w_down, b_down):
    nblk = row_tok.shape[0]
    assert nblk >= 2
    T, D = u.shape
    dff2 = w_gu.shape[2]
    dff = w_down.shape[1]
    grid_spec = pltpu.PrefetchScalarGridSpec(
        num_scalar_prefetch=2,
        grid=(nblk,),
        in_specs=[pl.BlockSpec(memory_space=pl.ANY),
                  pl.BlockSpec(memory_space=pl.ANY),
                  pl.BlockSpec((1, D, dff2), lambda i, be, nu: (be[i], 0, 0)),
                  pl.BlockSpec((1, 1, dff2), lambda i, be, nu: (be[i], 0, 0)),
                  pl.BlockSpec((1, dff, D), lambda i, be, nu: (be[i], 0, 0)),
                  pl.BlockSpec((1, 1, D), lambda i, be, nu: (be[i], 0, 0))],
        out_specs=pl.BlockSpec((MOE_ROWS, D), lambda i, be, nu: (i, 0)),
        scratch_shapes=[pltpu.SMEM((2, 1, MOE_ROWS), i32),
                        pltpu.VMEM((2, MOE_ROWS, D), f32),
                        pltpu.SemaphoreType.DMA((2,)),
                        pltpu.SemaphoreType.DMA((2,))],
    )
    return pl.pallas_call(
        _moe_body,
        grid_spec=grid_spec,
        out_shape=jax.ShapeDtypeStruct((nblk * MOE_ROWS, D), f32),
        compiler_params=_params("arbitrary"),
        name="moe_experts",
    )(blk_expert, n_used, row_tok, u, w_gu, b_gu, w_down, b_down)


def _combine_body(d_hbm, y_hbm, h_ref, w_ref, g_ref, o_ref, idx_smem, ybuf, isem, gsem, *, tm):
    slot = _gather_pipeline(_RowGather(d_hbm, y_hbm, idx_smem, ybuf, isem, gsem, TOP_K * tm))
    w = w_ref[...]
    mix = w[:, 0:1] * ybuf[slot, 0:tm, :]
    for k in range(1, TOP_K):
        mix = mix + w[:, k:k + 1] * ybuf[slot, k * tm:(k + 1) * tm, :]
    o_ref[...] = h_ref[...] + g_ref[0] * mix


def _moe_combine(dest_blk, y, h, wts_t, gate, S, tm):
    T, D = h.shape
    tpb = S // tm
    return pl.pallas_call(
        functools.partial(_combine_body, tm=tm),
        grid=(T // tm,),
        in_specs=[pl.BlockSpec(memory_space=pl.ANY),
                  pl.BlockSpec(memory_space=pl.ANY),
                  pl.BlockSpec((tm, D), lambda i: (i, 0)),
                  pl.BlockSpec((tm, TOP_K), lambda i: (i, 0)),
                  pl.BlockSpec((1, 1, D), lambda i: (i // tpb, 0, 0))],
        out_specs=pl.BlockSpec((tm, D), lambda i: (i, 0)),
        out_shape=jax.ShapeDtypeStruct((T, D), f32),
        scratch_shapes=[pltpu.SMEM((2, 1, TOP_K * tm), i32),
                        pltpu.VMEM((2, TOP_K * tm, D), f32),
                        pltpu.SemaphoreType.DMA((2,)),
                        pltpu.SemaphoreType.DMA((2,))],
        compiler_params=_params("arbitrary"),
        name="moe_combine",
    )(dest_blk, y, h, wts_t, gate)


def _moe_layer(h, nw, scale, shift, gate, w_router, b_router, w_gu, b_gu, w_down, b_down, S):
    T, D = h.shape
    tm_r = min(512, S)
    u, ids, wts, ranks, cnt = _router(h, nw, scale, shift, w_router.T, b_router.reshape(N_EXPERTS, 1),
                                      S, tm_r)
    counts = cnt[:, 0].astype(i32)
    padded = (counts + MOE_ROWS - 1) // MOE_ROWS * MOE_ROWS
    pad_end = jnp.cumsum(padded)
    pad_start = pad_end - padded
    eids = jnp.arange(N_EXPERTS, dtype=i32)[:, None, None]
    dest = jnp.sum(jnp.where(ids[None] == eids, pad_start[:, None, None], 0), axis=0) + ranks
    nblk = -(-(T * TOP_K + N_EXPERTS * (MOE_ROWS - 1)) // MOE_ROWS) + 1
    tok = jnp.broadcast_to(jnp.arange(T, dtype=i32)[None, :], (TOP_K, T))
    row_tok = jnp.zeros((nblk * MOE_ROWS,), i32).at[dest.reshape(-1)].set(
        tok.reshape(-1), unique_indices=True)
    blk_start = jnp.arange(nblk, dtype=i32) * MOE_ROWS
    blk_expert = jnp.minimum(jnp.sum((blk_start[:, None] >= pad_end[None, :]).astype(i32), axis=1),
                             N_EXPERTS - 1)
    n_used = (pad_end[-1:] // MOE_ROWS).astype(i32)
    y = _moe_experts(blk_expert, n_used, row_tok.reshape(nblk, 1, MOE_ROWS), u,
                     w_gu.astype(bf16), b_gu.reshape(N_EXPERTS, 1, -1),
                     w_down.astype(bf16), b_down.reshape(N_EXPERTS, 1, -1))
    tm_c = min(256, S)
    dest_blk = dest.reshape(TOP_K, T // tm_c, tm_c).transpose(1, 0, 2).reshape(T // tm_c, 1, TOP_K * tm_c)
    return _moe_combine(dest_blk, y, h, wts.T, gate, S, tm_c)


def _final_norm_body(h_ref, nw_ref, o_ref):
    x = h_ref[...]
    ms = jnp.mean(x * x, axis=-1, keepdims=True)
    o_ref[...] = x * lax.rsqrt(ms + RMS_EPS) * nw_ref[...]


def _final_norm(h, nw, tm):
    T, D = h.shape
    return pl.pallas_call(
        _final_norm_body,
        grid=(T // tm,),
        in_specs=[pl.BlockSpec((tm, D), lambda i: (i, 0)),
                  pl.BlockSpec((1, D), lambda i: (0, 0))],
        out_specs=pl.BlockSpec((tm, D), lambda i: (i, 0)),
        out_shape=jax.ShapeDtypeStruct((T, D), f32),
        compiler_params=_params("arbitrary"),
        name="final_norm",
    )(h, nw)


def kernel(x, c, w_ada, ada_table, norm_mix, norm_ffn, norm_final, lru_w_in, lru_conv_w, lru_conv_b, lru_w_rg, lru_b_rg, lru_w_ig, lru_b_ig, lru_lambda, lru_w_out, gdn_w_in, gdn_conv_w, gdn_a_log, gdn_dt_bias, gdn_norm, gdn_w_out, moe_w_router, moe_b_router, moe_w_gate_up, moe_b_gate_up, moe_w_down, moe_b_down):
    B, S, D = x.shape
    T = B * S
    depth = ada_table.shape[0]
    tm = min(1024, S)
    tt_lru = min(256, S)
    tt_gdn = min(256, S)

    cond = _ada(c, w_ada).reshape(B, N_MODS, D)
    h = x.reshape(T, D)
    for layer in range(depth):
        mods = cond + ada_table[layer]
        shift_m, scale_m, gate_m, shift_f, scale_f, gate_f = [mods[:, j:j + 1, :] for j in range(N_MODS)]
        nw = norm_mix[layer].reshape(1, D)
        j = layer // 2
        if layer % 2 == 0:
            proj = _norm_matmul(h, nw, scale_m, shift_m, lru_w_in[j].astype(bf16), None, S, tm, 1024)
            w_gates = jnp.concatenate([lru_w_rg[j], lru_w_ig[j]], axis=-1).astype(bf16)
            act = _lru_core(proj, lru_conv_w[j], lru_conv_b[j].reshape(1, D), w_gates,
                            lru_b_rg[j].reshape(1, D), lru_b_ig[j].reshape(1, D),
                            lru_lambda[j].reshape(1, D), B, S, tt_lru)
            h = _matmul_residual(act, lru_w_out[j].astype(bf16), h, gate_m, S, tm, 512)
        else:
            w_in = gdn_w_in[j]
            n_main = GDN_CONV_DIM + GDN_V_DIM
            pm, ba = _norm_matmul(h, nw, scale_m, shift_m, w_in[:, :n_main].astype(bf16),
                                  w_in[:, n_main:].astype(bf16), S, tm, 1024)
            beta, gc = _gdn_gates(ba, gdn_a_log[j].reshape(1, -1), gdn_dt_bias[j].reshape(1, -1),
                                  min(512, S))
            act = _gdn_core(pm, gdn_conv_w[j], gc, beta, gdn_norm[j].reshape(1, -1), B, S, tt_gdn, 4)
            h = _matmul_residual(act, gdn_w_out[j].astype(bf16), h, gate_m, S, tm, 512)
        h = _moe_layer(h, norm_ffn[layer].reshape(1, D), scale_f, shift_f, gate_f,
                       moe_w_router[layer], moe_b_router[layer], moe_w_gate_up[layer],
                       moe_b_gate_up[layer], moe_w_down[layer], moe_b_down[layer], S)
    return _final_norm(h, norm_final.reshape(1, D), tm).reshape(B, S, D)
```

```python
import functools

import jax
import jax.numpy as jnp
from jax import lax
from jax.experimental import pallas as pl
from jax.experimental.pallas import tpu as pltpu

f32 = jnp.float32
bf16 = jnp.bfloat16
i32 = jnp.int32

N_MODS = 6
CONV_WIDTH = 4
RMS_EPS = 1e-6
L2_EPS = 1e-6

LRU_HEADS = 16
LRU_HEAD_DIM = 128
LRU_C = 8.0

GDN_HEAD_DIM = 128
GDN_KEY_HEADS = 16
GDN_VALUE_HEADS = 32
GDN_QK_DIM = GDN_KEY_HEADS * GDN_HEAD_DIM
GDN_V_DIM = GDN_VALUE_HEADS * GDN_HEAD_DIM
GDN_CONV_DIM = 2 * GDN_QK_DIM + GDN_V_DIM
CHUNK = 64

N_EXPERTS = 32
TOP_K = 4
SWIGLU_ALPHA = 1.702
SWIGLU_LIMIT = 7.0
MOE_ROWS = 256

VMEM_LIMIT_BYTES = 56 * 1024 * 1024
SUBLANES = 8


def _params(*sem):
    return pltpu.CompilerParams(dimension_semantics=sem, vmem_limit_bytes=VMEM_LIMIT_BYTES)


def _dot(a, b):
    return jnp.dot(a.astype(bf16), b.astype(bf16), preferred_element_type=f32)


def _dot_nt(a, b):
    return lax.dot_general(a.astype(bf16), b.astype(bf16), (((1,), (1,)), ((), ())),
                           preferred_element_type=f32)


def _dot_tn(a, b):
    return lax.dot_general(a.astype(bf16), b.astype(bf16), (((0,), (0,)), ((), ())),
                           preferred_element_type=f32)


def _silu(x):
    return x * jax.nn.sigmoid(x)


def _softplus(x):
    return jnp.maximum(x, 0.0) + jnp.log(1.0 + jnp.exp(-jnp.abs(x)))


def _modulated_norm(x, nw, scale, shift):
    ms = jnp.mean(x * x, axis=-1, keepdims=True)
    y = x * lax.rsqrt(ms + RMS_EPS) * nw
    return y * (1.0 + scale) + shift


def _ada_body(c_ref, w_ref, o_ref):
    o_ref[...] = _dot(_silu(c_ref[...]), w_ref[...])


def _ada(c, w_ada):
    B, D = c.shape
    N = w_ada.shape[1]
    tn = 1024
    return pl.pallas_call(
        _ada_body,
        grid=(N // tn,),
        in_specs=[pl.BlockSpec((B, D), lambda j: (0, 0)),
                  pl.BlockSpec((D, tn), lambda j: (0, j))],
        out_specs=pl.BlockSpec((B, tn), lambda j: (0, j)),
        out_shape=jax.ShapeDtypeStruct((B, N), f32),
        compiler_params=_params("arbitrary"),
        name="ada",
    )(c, w_ada)


def _norm_mm_body(h_ref, nw_ref, sc_ref, sh_ref, w_ref, cw_ref, cb_ref, *rest,
                  has_small, conv_tiles, conv_silu, tpb):
    if has_small:
        ws_ref, o_ref, os_ref, u_ref, pad_ref, tail_ref = rest
    else:
        o_ref, u_ref, pad_ref, tail_ref = rest
    i = pl.program_id(0)
    j = pl.program_id(1)
    tm = o_ref.shape[0]

    @pl.when(j == 0)
    def _():
        u = _modulated_norm(h_ref[...], nw_ref[...], sc_ref[0], sh_ref[0]).astype(bf16)
        u_ref[...] = u
        if has_small:
            os_ref[...] = jnp.dot(u, ws_ref[...], preferred_element_type=f32)

    acc = jnp.dot(u_ref[...], w_ref[...], preferred_element_type=f32)

    @pl.when(j < conv_tiles)
    def _():
        jj = jnp.minimum(j, conv_tiles - 1)
        first = lax.rem(i, tpb) == 0

        @pl.when(first)
        def _():
            pad_ref[0:SUBLANES, :] = jnp.zeros((SUBLANES, pad_ref.shape[1]), f32)

        @pl.when(jnp.logical_not(first))
        def _():
            pad_ref[0:SUBLANES, :] = tail_ref[jj]

        pad_ref[SUBLANES:SUBLANES + tm, :] = acc
        base = SUBLANES - (CONV_WIDTH - 1)
        y = cb_ref[...] + cw_ref[0:1, :] * pad_ref[pl.ds(base, tm), :]
        for k in range(1, CONV_WIDTH):
            y = y + cw_ref[k:k + 1, :] * pad_ref[pl.ds(base + k, tm), :]
        tail_ref[jj] = pad_ref[tm:tm + SUBLANES, :]
        o_ref[...] = _silu(y) if conv_silu else y

    @pl.when(j >= conv_tiles)
    def _():
        o_ref[...] = acc


def _norm_matmul(h, nw, scale, shift, w, w_small, conv_w, conv_b, conv_silu, S, tm, tn):
    T, D = h.shape
    N = w.shape[1]
    tpb = S // tm
    conv_tiles = conv_w.shape[1] // tn
    assert conv_tiles * tn == conv_w.shape[1] and conv_tiles >= 1
    cj = lambda i, j: (0, jnp.minimum(j, conv_tiles - 1))
    in_specs = [pl.BlockSpec((tm, D), lambda i, j: (i, 0)),
                pl.BlockSpec((1, D), lambda i, j: (0, 0)),
                pl.BlockSpec((1, 1, D), lambda i, j: (i // tpb, 0, 0)),
                pl.BlockSpec((1, 1, D), lambda i, j: (i // tpb, 0, 0)),
                pl.BlockSpec((D, tn), lambda i, j: (0, j)),
                pl.BlockSpec((CONV_WIDTH, tn), cj),
                pl.BlockSpec((1, tn), cj)]
    out_specs = [pl.BlockSpec((tm, tn), lambda i, j: (i, j))]
    out_shape = [jax.ShapeDtypeStruct((T, N), f32)]
    args = [h, nw, scale, shift, w, conv_w, conv_b]
    if w_small is not None:
        ns = w_small.shape[1]
        in_specs.append(pl.BlockSpec((D, ns), lambda i, j: (0, 0)))
        out_specs.append(pl.BlockSpec((tm, ns), lambda i, j: (i, 0)))
        out_shape.append(jax.ShapeDtypeStruct((T, ns), f32))
        args.append(w_small)
    res = pl.pallas_call(
        functools.partial(_norm_mm_body, has_small=w_small is not None, conv_tiles=conv_tiles,
                          conv_silu=conv_silu, tpb=tpb),
        grid=(T // tm, N // tn),
        in_specs=in_specs, out_specs=out_specs, out_shape=out_shape,
        scratch_shapes=[pltpu.VMEM((tm, D), bf16),
                        pltpu.VMEM((tm + SUBLANES, tn), f32),
                        pltpu.VMEM((conv_tiles, SUBLANES, tn), f32)],
        compiler_params=_params("arbitrary", "arbitrary"),
        name="norm_matmul",
    )(*args)
    return res if w_small is not None else res[0]


def _mm_res_body(a_ref, w_ref, h_ref, g_ref, o_ref):
    acc = jnp.dot(a_ref[...], w_ref[...], preferred_element_type=f32)
    o_ref[...] = h_ref[...] + g_ref[0] * acc


def _matmul_residual(a, w, h, gate, S, tm, tn):
    T, K = a.shape
    D = w.shape[1]
    tpb = S // tm
    return pl.pallas_call(
        _mm_res_body,
        grid=(T // tm, D // tn),
        in_specs=[pl.BlockSpec((tm, K), lambda i, j: (i, 0)),
                  pl.BlockSpec((K, tn), lambda i, j: (0, j)),
                  pl.BlockSpec((tm, tn), lambda i, j: (i, j)),
                  pl.BlockSpec((1, 1, tn), lambda i, j: (i // tpb, 0, j))],
        out_specs=pl.BlockSpec((tm, tn), lambda i, j: (i, j)),
        out_shape=jax.ShapeDtypeStruct((T, D), f32),
        compiler_params=_params("arbitrary", "arbitrary"),
        name="matmul_residual",
    )(a, w, h, gate)


def _lru_body(xb_ref, yb_ref, wg_ref, brg_ref, big_ref, lam_ref, o_ref,
              hstate_ref, a_ref, b_ref, hs_ref, *, tt):
    @pl.when(pl.program_id(1) == 0)
    def _():
        hstate_ref[...] = jnp.zeros_like(hstate_ref)

    xc = xb_ref[...]
    neg_c_sp = -LRU_C * _softplus(-lam_ref[...])
    hd = LRU_HEAD_DIM
    for hh in range(LRU_HEADS):
        sl = slice(hh * hd, (hh + 1) * hd)
        xh = xc[:, sl]
        gates = _dot(xh, wg_ref[hh])
        r = jax.nn.sigmoid(gates[:, :hd] + brg_ref[:, sl])
        ig = jax.nn.sigmoid(gates[:, hd:] + big_ref[:, sl])
        a = jnp.exp(r * neg_c_sp[:, sl])
        a_ref[:, sl] = a
        b_ref[:, sl] = jnp.sqrt(jnp.maximum(1.0 - a * a, 0.0)) * (ig * xh)

    def step(i, h):
        h = a_ref[pl.ds(i, 1), :] * h + b_ref[pl.ds(i, 1), :]
        hs_ref[pl.ds(i, 1), :] = h
        return h

    hstate_ref[...] = lax.fori_loop(0, tt, step, hstate_ref[...], unroll=8)
    o_ref[...] = (hs_ref[...] * jax.nn.gelu(yb_ref[...].astype(f32), approximate=True)).astype(o_ref.dtype)


def _lru_core(proj, w_gates, b_rg, b_ig, lam, B, S, tt):
    T = proj.shape[0]
    D = proj.shape[1] // 2
    nt = S // tt
    vec = pl.BlockSpec((1, D), lambda b, t: (0, 0))
    return pl.pallas_call(
        functools.partial(_lru_body, tt=tt),
        grid=(B, nt),
        in_specs=[pl.BlockSpec((tt, D), lambda b, t: (b * nt + t, 0)),
                  pl.BlockSpec((tt, D), lambda b, t: (b * nt + t, 1)),
                  pl.BlockSpec((LRU_HEADS, LRU_HEAD_DIM, 2 * LRU_HEAD_DIM), lambda b, t: (0, 0, 0)),
                  vec, vec, vec],
        out_specs=pl.BlockSpec((tt, D), lambda b, t: (b * nt + t, 0)),
        out_shape=jax.ShapeDtypeStruct((T, D), bf16),
        scratch_shapes=[pltpu.VMEM((1, D), f32),
                        pltpu.VMEM((tt, D), f32),
                        pltpu.VMEM((tt, D), f32),
                        pltpu.VMEM((tt, D), f32)],
        compiler_params=_params("arbitrary", "arbitrary"),
        name="lru_core",
    )(proj, proj, w_gates, b_rg, b_ig, lam)


def _gdn_gate_body(ba_ref, alog_ref, dtb_ref, beta_ref, gc_ref, *, tt):
    nh = GDN_VALUE_HEADS
    ba = ba_ref[...]
    beta_ref[...] = jax.nn.sigmoid(ba[:, :nh])
    g = -jnp.exp(alog_ref[...]) * _softplus(ba[:, nh:] + dtb_ref[...])
    ri = lax.broadcasted_iota(i32, (tt, tt), 0)
    ci = lax.broadcasted_iota(i32, (tt, tt), 1)
    same_chunk = (ri // CHUNK) == (ci // CHUNK)
    tri = jnp.where(same_chunk & (ci <= ri), 1.0, 0.0).astype(f32)
    gc_ref[...] = jnp.dot(tri, g, precision=lax.Precision.HIGHEST, preferred_element_type=f32)


def _gdn_gates(ba, a_log, dt_bias, tt):
    T = ba.shape[0]
    nh = GDN_VALUE_HEADS
    out = jax.ShapeDtypeStruct((T, nh), f32)
    return pl.pallas_call(
        functools.partial(_gdn_gate_body, tt=tt),
        grid=(T // tt,),
        in_specs=[pl.BlockSpec((tt, 2 * nh), lambda i: (i, 0)),
                  pl.BlockSpec((1, nh), lambda i: (0, 0)),
                  pl.BlockSpec((1, nh), lambda i: (0, 0))],
        out_specs=[pl.BlockSpec((tt, nh), lambda i: (i, 0)),
                   pl.BlockSpec((tt, nh), lambda i: (i, 0))],
        out_shape=[out, out],
        compiler_params=_params("arbitrary"),
        name="gdn_gates",
    )(ba, a_log, dt_bias)


def _gdn_body(q_ref, k_ref, v_ref, z_ref, gc_ref, be_ref, nw_ref, o_ref, state_ref, *, G, tt):
    hd = GDN_HEAD_DIM
    group = GDN_VALUE_HEADS // GDN_KEY_HEADS

    @pl.when(pl.program_id(2) == 0)
    def _():
        state_ref[...] = jnp.zeros_like(state_ref)

    q = q_ref[...]
    k = k_ref[...]
    v = v_ref[...]
    gcb = gc_ref[...]
    beb = be_ref[...]
    nchunk = tt // CHUNK
    heads = G * group
    pw_rows = group * CHUNK
    assert pw_rows == hd
    ri = lax.broadcasted_iota(i32, (pw_rows, pw_rows), 0)
    ci = lax.broadcasted_iota(i32, (pw_rows, pw_rows), 1)
    same_head = (ri // CHUNK) == (ci // CHUNK)
    causal = same_head & (ri >= ci)
    strict = same_head & (ri > ci)
    eye = ri == ci
    col_head = lax.broadcasted_iota(i32, (1, pw_rows), 1) // CHUNK

    ones_hd = jnp.ones((hd, hd), bf16)

    def row_sumsq(x):
        return jnp.dot((x * x).astype(bf16), ones_hd, preferred_element_type=f32)

    qn, kn = [], []
    for kh in range(G):
        qh = q[:, kh * hd:(kh + 1) * hd]
        kk = k[:, kh * hd:(kh + 1) * hd]
        qn.append(qh * (lax.rsqrt(row_sumsq(qh) + L2_EPS) * (hd ** -0.5)))
        kn.append(kk * lax.rsqrt(row_sumsq(kk) + L2_EPS))
    pick_shape = (GDN_VALUE_HEADS, heads * hd)
    picked_head = pl.program_id(1) * heads + lax.broadcasted_iota(i32, pick_shape, 1) // hd
    pick = jnp.where(lax.broadcasted_iota(i32, pick_shape, 0) == picked_head, 1.0, 0.0).astype(bf16)
    g_hi = gcb.astype(bf16)
    g_lo = (gcb - g_hi.astype(f32)).astype(bf16)
    g_rep = (jnp.dot(g_hi, pick, preferred_element_type=f32)
             + jnp.dot(g_lo, pick, preferred_element_type=f32))
    b_rep = jnp.dot(beb.astype(bf16), pick, preferred_element_type=f32)
    g_all = [g_rep[:, hl * hd:(hl + 1) * hd] for hl in range(heads)]
    b_all = [b_rep[:, hl * hd:(hl + 1) * hd] for hl in range(heads)]

    def stack_heads(per_head):
        return jnp.concatenate(per_head, axis=0)

    items = [(kh, c) for c in range(nchunk) for kh in range(G)]
    rows_of = lambda c: slice(c * CHUNK, (c + 1) * CHUNK)
    decay, a_mat, xs, q_dec, k_dec_t, s_gain, qk = {}, {}, {}, {}, {}, {}, {}
    for p in items:
        kh, c = p
        hs = [kh * group + j for j in range(group)]
        kc = kn[kh][rows_of(c)]
        k2 = stack_heads([kc] * group)
        q2 = stack_heads([qn[kh][rows_of(c)]] * group)
        kt = kc.T
        kt2 = jnp.concatenate([kt] * group, axis=1)
        gcol = stack_heads([g_all[h][rows_of(c)] for h in hs])
        bcol = stack_heads([b_all[h][rows_of(c)] for h in hs])
        grow = jnp.sum(jnp.where(eye, gcol, 0.0), axis=0, keepdims=True)
        decay[p] = jnp.where(causal, jnp.exp(gcol - grow), 0.0)
        a_mat[p] = jnp.where(strict, bcol * _dot(k2, kt2) * decay[p], 0.0)
        qk[p] = _dot(q2, kt2) * decay[p]
        eg = jnp.exp(gcol)
        q_dec[p] = q2 * eg
        v2 = stack_heads([v[:, h * hd:(h + 1) * hd][rows_of(c)] for h in hs])
        xs[p] = jnp.concatenate([v2 * bcol, k2 * (bcol * eg)], axis=1)
        lasts = [g_all[h][c * CHUNK + CHUNK - 1:(c + 1) * CHUNK, :] for h in hs]
        gl_row = lasts[0]
        for j in range(1, group):
            gl_row = jnp.where(col_head == j, lasts[j], gl_row)
        for j in range(group):
            s_gain[kh * group + j, c] = jnp.exp(lasts[j])
        kd = kt2 * jnp.exp(gl_row - grow)
        k_dec_t[p] = stack_heads([jnp.where(col_head == j, kd, 0.0) for j in range(group)])
    for p in items:
        xs[p] = xs[p] - _dot(a_mat[p], xs[p])
    pw = dict(a_mat)
    span = 2
    while span < CHUNK:
        for p in items:
            pw[p] = _dot(pw[p], pw[p])
        for p in items:
            xs[p] = xs[p] + _dot(pw[p], xs[p])
        span *= 2
    lhs, o_add, s_add = {}, {}, {}
    for p in items:
        kh, c = p
        pq = _dot(qk[p], xs[p])
        mk = _dot(k_dec_t[p], xs[p])
        q_eff = q_dec[p] - pq[:, hd:]
        for j in range(group):
            n = (kh * group + j, c)
            o_add[n] = pq[j * CHUNK:(j + 1) * CHUNK, :hd]
            s_add[n] = mk[j * hd:(j + 1) * hd, :hd]
            lhs[n] = jnp.concatenate([q_eff[j * CHUNK:(j + 1) * CHUNK], mk[j * hd:(j + 1) * hd, hd:]], axis=0)
    st = [state_ref[hl] for hl in range(heads)]
    outs = [[] for _ in range(heads)]
    for c in range(nchunk):
        for hl in range(heads):
            n = (hl, c)
            r = _dot(lhs[n], st[hl])
            outs[hl].append(r[:CHUNK] + o_add[n])
            st[hl] = st[hl] * s_gain[n] - r[CHUNK:] + s_add[n]
    for hl in range(heads):
        state_ref[hl] = st[hl]
        o_all = jnp.concatenate(outs[hl], axis=0)
        zz = z_ref[:, hl * hd:(hl + 1) * hd].astype(f32)
        on = o_all * lax.rsqrt(row_sumsq(o_all) * (1.0 / hd) + RMS_EPS) * nw_ref[...]
        o_ref[:, hl * hd:(hl + 1) * hd] = (on * _silu(zz)).astype(o_ref.dtype)


def _gdn_core(pm, gc, beta, norm_w, B, S, tt, G):
    T = pm.shape[0]
    hd = GDN_HEAD_DIM
    nt = S // tt
    qw = G * hd
    vw = 2 * G * hd
    nq = GDN_QK_DIM // qw
    nv = GDN_V_DIM // vw
    row = lambda b, g, t: b * nt + t
    return pl.pallas_call(
        functools.partial(_gdn_body, G=G, tt=tt),
        grid=(B, GDN_KEY_HEADS // G, nt),
        in_specs=[pl.BlockSpec((tt, qw), lambda b, g, t: (row(b, g, t), g)),
                  pl.BlockSpec((tt, qw), lambda b, g, t: (row(b, g, t), nq + g)),
                  pl.BlockSpec((tt, vw), lambda b, g, t: (row(b, g, t), nv + g)),
                  pl.BlockSpec((tt, vw), lambda b, g, t: (row(b, g, t), 2 * nv + g)),
                  pl.BlockSpec((tt, GDN_VALUE_HEADS), lambda b, g, t: (row(b, g, t), 0)),
                  pl.BlockSpec((tt, GDN_VALUE_HEADS), lambda b, g, t: (row(b, g, t), 0)),
                  pl.BlockSpec((1, hd), lambda b, g, t: (0, 0))],
        out_specs=pl.BlockSpec((tt, vw), lambda b, g, t: (row(b, g, t), g)),
        out_shape=jax.ShapeDtypeStruct((T, GDN_V_DIM), bf16),
        scratch_shapes=[pltpu.VMEM((2 * G, hd, hd), f32)],
        compiler_params=_params("arbitrary", "arbitrary", "arbitrary"),
        name="gdn_core",
    )(pm, pm, pm, pm, gc, beta, norm_w)


def _router_body(h_ref, nw_ref, sc_ref, sh_ref, wr_ref, br_ref,
                 u_ref, ids_ref, wts_ref, rk_ref, cnt_ref, *, tm):
    @pl.when(pl.program_id(0) == 0)
    def _():
        cnt_ref[...] = jnp.zeros_like(cnt_ref)

    u = _modulated_norm(h_ref[...], nw_ref[...], sc_ref[0], sh_ref[0])
    u_ref[...] = u
    logits = lax.dot_general(wr_ref[...], u, (((1,), (1,)), ((), ())),
                             precision=lax.Precision.HIGHEST,
                             preferred_element_type=f32) + br_ref[...]
    eidx = lax.broadcasted_iota(i32, logits.shape, 0)
    cur = logits
    vals, sels = [], []
    for _ in range(TOP_K):
        m = jnp.max(cur, axis=0, keepdims=True)
        sel = jnp.min(jnp.where(cur == m, eidx, N_EXPERTS), axis=0, keepdims=True)
        vals.append(m)
        sels.append(sel)
        cur = jnp.where(eidx == sel, -jnp.inf, cur)
    ex = [jnp.exp(v - vals[0]) for v in vals]
    den = ex[0] + ex[1] + ex[2] + ex[3]
    wts_ref[...] = jnp.concatenate([e / den for e in ex], axis=0)
    ids_ref[...] = jnp.concatenate(sels, axis=0)
    onehots = [eidx == s for s in sels]
    chosen = jnp.zeros(logits.shape, f32)
    for oh in onehots:
        chosen = chosen + jnp.where(oh, 1.0, 0.0)
    ti = lax.broadcasted_iota(i32, (tm, tm), 0)
    tj = lax.broadcasted_iota(i32, (tm, tm), 1)
    before = jnp.where(ti < tj, 1.0, 0.0)
    base = _dot(chosen, before) + cnt_ref[:, 0:1]
    ranks = [jnp.sum(jnp.where(oh, base, 0.0), axis=0, keepdims=True) for oh in onehots]
    rk_ref[...] = jnp.concatenate(ranks, axis=0).astype(i32)
    cnt_ref[...] = cnt_ref[...] + jnp.sum(chosen, axis=1, keepdims=True)


def _router(h, nw, scale, shift, w_r_t, b_r, S, tm):
    T, D = h.shape
    tpb = S // tm
    lanes = 128
    return pl.pallas_call(
        functools.partial(_router_body, tm=tm),
        grid=(T // tm,),
        in_specs=[pl.BlockSpec((tm, D), lambda i: (i, 0)),
                  pl.BlockSpec((1, D), lambda i: (0, 0)),
                  pl.BlockSpec((1, 1, D), lambda i: (i // tpb, 0, 0)),
                  pl.BlockSpec((1, 1, D), lambda i: (i // tpb, 0, 0)),
                  pl.BlockSpec((N_EXPERTS, D), lambda i: (0, 0)),
                  pl.BlockSpec((N_EXPERTS, 1), lambda i: (0, 0))],
        out_specs=[pl.BlockSpec((tm, D), lambda i: (i, 0)),
                   pl.BlockSpec((TOP_K, tm), lambda i: (0, i)),
                   pl.BlockSpec((TOP_K, tm), lambda i: (0, i)),
                   pl.BlockSpec((TOP_K, tm), lambda i: (0, i)),
                   pl.BlockSpec((N_EXPERTS, lanes), lambda i: (0, 0))],
        out_shape=[jax.ShapeDtypeStruct((T, D), f32),
                   jax.ShapeDtypeStruct((TOP_K, T), i32),
                   jax.ShapeDtypeStruct((TOP_K, T), f32),
                   jax.ShapeDtypeStruct((TOP_K, T), i32),
                   jax.ShapeDtypeStruct((N_EXPERTS, lanes), f32)],
        compiler_params=_params("arbitrary"),
        name="moe_router",
    )(h, nw, scale, shift, w_r_t, b_r)


ISSUE_GROUP = 16


class _RowGather:
    def __init__(self, idx_hbm, src_hbm, idx_smem, buf, isem, gsem, n_idx):
        self.idx_hbm, self.src_hbm, self.idx_smem, self.buf = idx_hbm, src_hbm, idx_smem, buf
        self.isem, self.gsem, self.n_idx = isem, gsem, n_idx

    def idx_copy(self, blk, s):
        return pltpu.make_async_copy(self.idx_hbm.at[blk], self.idx_smem.at[s], self.isem.at[s])

    def row_copy(self, s, r, row):
        return pltpu.make_async_copy(self.src_hbm.at[pl.ds(row, 1), :],
                                     self.buf.at[s, pl.ds(r, 1), :], self.gsem.at[s])

    def issue_range(self, s, r0, n):
        rows = [self.idx_smem[s, 0, r0 + j] for j in range(n)]
        for j in range(n):
            self.row_copy(s, r0 + j, rows[j]).start()

    def issue_rows(self, s):
        def body(g, carry):
            self.issue_range(s, pl.multiple_of(g * ISSUE_GROUP, ISSUE_GROUP), ISSUE_GROUP)
            return carry
        lax.fori_loop(0, self.n_idx // ISSUE_GROUP, body, 0)

    def issue_rows_unrolled(self, s):
        for g in range(self.n_idx // ISSUE_GROUP):
            self.issue_range(s, g * ISSUE_GROUP, ISSUE_GROUP)

    def wait_rows(self, s):
        def body(r, carry):
            self.row_copy(s, r, 0).wait()
            return carry
        lax.fori_loop(0, self.n_idx, body, 0, unroll=8)


def _gather_pipeline(gather):
    i = pl.program_id(0)
    nblk = pl.num_programs(0)
    slot = lax.rem(i, 2)
    nslot = 1 - slot

    @pl.when(i == 0)
    def _():
        first = gather.idx_copy(0, 0)
        first.start()
        first.wait()
        gather.issue_rows(0)

        @pl.when(nblk > 1)
        def _():
            gather.idx_copy(1, 1).start()

    for par in (0, 1):
        @pl.when(jnp.logical_and(i + 1 < nblk, slot == par))
        def _(par=par):
            gather.idx_copy(i + 1, 1 - par).wait()
            gather.issue_rows_unrolled(1 - par)

    @pl.when(i + 2 < nblk)
    def _():
        gather.idx_copy(i + 2, slot).start()

    gather.wait_rows(slot)
    return slot


MOE_NTILE = 256
MOE_ISSUE_TILES = 4


def _moe_body(be_ref, nu_ref, rt_hbm, u_hbm, wgu_ref, bgu_ref, wd_ref, bd_ref, y_ref,
              idx_smem, xbuf, isem, gsem):
    gather = _RowGather(rt_hbm, u_hbm, idx_smem, xbuf, isem, gsem, MOE_ROWS)
    i = pl.program_id(0)
    n_used = nu_ref[0]
    used = i < n_used
    slot = lax.rem(i, 2)
    nslot = 1 - slot

    @pl.when(i == 0)
    def _():
        first = gather.idx_copy(0, 0)
        first.start()
        first.wait()
        gather.issue_rows(0)
        gather.idx_copy(1, 1).start()

    @pl.when(i <= n_used)
    def _():
        gather.wait_rows(slot)

    def multiply(slot, nslot):
        gather.idx_copy(i + 1, nslot).wait()
        dff = wd_ref.shape[1]
        d_out = wd_ref.shape[2]
        n_up = 2 * dff // MOE_NTILE
        n_down = d_out // MOE_NTILE
        n_chunks = min(MOE_ISSUE_TILES, n_up)
        bounds = [(MOE_ROWS * c) // n_chunks for c in range(n_chunks + 1)]

        def issue_chunk(c):
            if c < n_chunks:
                gather.issue_range(nslot, bounds[c], bounds[c + 1] - bounds[c])

        x = xbuf[slot].astype(bf16)
        gu = []
        for j in range(n_up):
            issue_chunk(j)
            cols = slice(j * MOE_NTILE, (j + 1) * MOE_NTILE)
            gu.append(jnp.dot(x, wgu_ref[0, :, cols], preferred_element_type=f32) + bgu_ref[0, :, cols])
        gu = jnp.concatenate(gu, axis=1)
        glu = jnp.minimum(gu[:, :dff], SWIGLU_LIMIT)
        lin = jnp.clip(gu[:, dff:], -SWIGLU_LIMIT, SWIGLU_LIMIT)
        hid = (glu * jax.nn.sigmoid(SWIGLU_ALPHA * glu) * (lin + 1.0)).astype(bf16)
        for j in range(n_down):
            issue_chunk(n_up + j)
            cols = slice(j * MOE_NTILE, (j + 1) * MOE_NTILE)
            y_ref[:, cols] = jnp.dot(hid, wd_ref[0, :, cols], preferred_element_type=f32) + bd_ref[0, :, cols]

    for par in (0, 1):
        @pl.when(jnp.logical_and(used, slot == par))
        def _(par=par):
            multiply(par, 1 - par)

    @pl.when(i + 1 < n_used)
    def _():
        gather.idx_copy(i + 2, slot).start()

    @pl.when(jnp.logical_not(used))
    def _():
        y_ref[...] = jnp.zeros_like(y_ref)


def _moe_experts(blk_expert, n_used, row_tok, u, w_gu, b_gu, w_down, b_down):
    nblk = row_tok.shape[0]
    assert nblk >= 2
    T, D = u.shape
    dff2 = w_gu.shape[2]
    dff = w_down.shape[1]
    grid_spec = pltpu.PrefetchScalarGridSpec(
        num_scalar_prefetch=2,
        grid=(nblk,),
        in_specs=[pl.BlockSpec(memory_space=pl.ANY),
                  pl.BlockSpec(memory_space=pl.ANY),
                  pl.BlockSpec((1, D, dff2), lambda i, be, nu: (be[i], 0, 0)),
                  pl.BlockSpec((1, 1, dff2), lambda i, be, nu: (be[i], 0, 0)),
                  pl.BlockSpec((1, dff, D), lambda i, be, nu: (be[i], 0, 0)),
                  pl.BlockSpec((1, 1, D), lambda i, be, nu: (be[i], 0, 0))],
        out_specs=pl.BlockSpec((MOE_ROWS, D), lambda i, be, nu: (i, 0)),
        scratch_shapes=[pltpu.SMEM((2, 1, MOE_ROWS), i32),
                        pltpu.VMEM((2, MOE_ROWS, D), f32),
                        pltpu.SemaphoreType.DMA((2,)),
                        pltpu.SemaphoreType.DMA((2,))],
    )
    return pl.pallas_call(
        _moe_body,
        grid_spec=grid_spec,
        out_shape=jax.ShapeDtypeStruct((nblk * MOE_ROWS, D), f32),
        compiler_params=_params("arbitrary"),
        name="moe_experts",
    )(blk_expert, n_used, row_tok, u, w_gu, b_gu, w_down, b_down)


def _combine_body(d_hbm, y_hbm, h_ref, w_ref, g_ref, o_ref, idx_smem, ybuf, isem, gsem, *, tm):
    slot = _gather_pipeline(_RowGather(d_hbm, y_hbm, idx_smem, ybuf, isem, gsem, TOP_K * tm))
    w = w_ref[...]
    mix = w[:, 0:1] * ybuf[slot, 0:tm, :]
    for k in range(1, TOP_K):
        mix = mix + w[:, k:k + 1] * ybuf[slot, k * tm:(k + 1) * tm, :]
    o_ref[...] = h_ref[...] + g_ref[0] * mix


def _moe_combine(dest_blk, y, h, wts_t, gate, S, tm):
    T, D = h.shape
    tpb = S // tm
    return pl.pallas_call(
        functools.partial(_combine_body, tm=tm),
        grid=(T // tm,),
        in_specs=[pl.BlockSpec(memory_space=pl.ANY),
                  pl.BlockSpec(memory_space=pl.ANY),
                  pl.BlockSpec((tm, D), lambda i: (i, 0)),
                  pl.BlockSpec((tm, TOP_K), lambda i: (i, 0)),
                  pl.BlockSpec((1, 1, D), lambda i: (i // tpb, 0, 0))],
        out_specs=pl.BlockSpec((tm, D), lambda i: (i, 0)),
        out_shape=jax.ShapeDtypeStruct((T, D), f32),
        scratch_shapes=[pltpu.SMEM((2, 1, TOP_K * tm), i32),
                        pltpu.VMEM((2, TOP_K * tm, D), f32),
                        pltpu.SemaphoreType.DMA((2,)),
                        pltpu.SemaphoreType.DMA((2,))],
        compiler_params=_params("arbitrary"),
        name="moe_combine",
    )(dest_blk, y, h, wts_t, gate)


def _moe_layer(h, nw, scale, shift, gate, w_router, b_router, w_gu, b_gu, w_down, b_down, S):
    T, D = h.shape
    tm_r = min(512, S)
    u, ids, wts, ranks, cnt = _router(h, nw, scale, shift, w_router.T, b_router.reshape(N_EXPERTS, 1),
                                      S, tm_r)
    counts = cnt[:, 0].astype(i32)
    padded = (counts + MOE_ROWS - 1) // MOE_ROWS * MOE_ROWS
    pad_end = jnp.cumsum(padded)
    pad_start = pad_end - padded
    eids = jnp.arange(N_EXPERTS, dtype=i32)[:, None, None]
    dest = jnp.sum(jnp.where(ids[None] == eids, pad_start[:, None, None], 0), axis=0) + ranks
    nblk = -(-(T * TOP_K + N_EXPERTS * (MOE_ROWS - 1)) // MOE_ROWS) + 1
    tok = jnp.broadcast_to(jnp.arange(T, dtype=i32)[None, :], (TOP_K, T))
    row_tok = jnp.zeros((nblk * MOE_ROWS,), i32).at[dest.reshape(-1)].set(
        tok.reshape(-1), unique_indices=True)
    blk_start = jnp.arange(nblk, dtype=i32) * MOE_ROWS
    blk_expert = jnp.minimum(jnp.sum((blk_start[:, None] >= pad_end[None, :]).astype(i32), axis=1),
                             N_EXPERTS - 1)
    n_used = (pad_end[-1:] // MOE_ROWS).astype(i32)
    y = _moe_experts(blk_expert, n_used, row_tok.reshape(nblk, 1, MOE_ROWS), u,
                     w_gu.astype(bf16), b_gu.reshape(N_EXPERTS, 1, -1),
                     w_down.astype(bf16), b_down.reshape(N_EXPERTS, 1, -1))
    tm_c = min(256, S)
    dest_blk = dest.reshape(TOP_K, T // tm_c, tm_c).transpose(1, 0, 2).reshape(T // tm_c, 1, TOP_K * tm_c)
    return _moe_combine(dest_blk, y, h, wts.T, gate, S, tm_c)


def _final_norm_body(h_ref, nw_ref, o_ref):
    x = h_ref[...]
    ms = jnp.mean(x * x, axis=-1, keepdims=True)
    o_ref[...] = x * lax.rsqrt(ms + RMS_EPS) * nw_ref[...]


def _final_norm(h, nw, tm):
    T, D = h.shape
    return pl.pallas_call(
        _final_norm_body,
        grid=(T // tm,),
        in_specs=[pl.BlockSpec((tm, D), lambda i: (i, 0)),
                  pl.BlockSpec((1, D), lambda i: (0, 0))],
        out_specs=pl.BlockSpec((tm, D), lambda i: (i, 0)),
        out_shape=jax.ShapeDtypeStruct((T, D), f32),
        compiler_params=_params("arbitrary"),
        name="final_norm",
    )(h, nw)


def kernel(x, c, w_ada, ada_table, norm_mix, norm_ffn, norm_final, lru_w_in, lru_conv_w, lru_conv_b, lru_w_rg, lru_b_rg, lru_w_ig, lru_b_ig, lru_lambda, lru_w_out, gdn_w_in, gdn_conv_w, gdn_a_log, gdn_dt_bias, gdn_norm, gdn_w_out, moe_w_router, moe_b_router, moe_w_gate_up, moe_b_gate_up, moe_w_down, moe_b_down):
    B, S, D = x.shape
    T = B * S
    depth = ada_table.shape[0]
    tm = min(1024, S)
    tt_lru = min(256, S)
    tt_gdn = min(256, S)

    cond = _ada(c, w_ada).reshape(B, N_MODS, D)
    h = x.reshape(T, D)
    for layer in range(depth):
        mods = cond + ada_table[layer]
        shift_m, scale_m, gate_m, shift_f, scale_f, gate_f = [mods[:, j:j + 1, :] for j in range(N_MODS)]
        nw = norm_mix[layer].reshape(1, D)
        j = layer // 2
        if layer % 2 == 0:
            proj = _norm_matmul(h, nw, scale_m, shift_m, lru_w_in[j].astype(bf16), None,
                                lru_conv_w[j], lru_conv_b[j].reshape(1, D), False, S, tm, 1024)
            w_gates = jnp.concatenate([lru_w_rg[j], lru_w_ig[j]], axis=-1).astype(bf16)
            act = _lru_core(proj, w_gates, lru_b_rg[j].reshape(1, D), lru_b_ig[j].reshape(1, D),
                            lru_lambda[j].reshape(1, D), B, S, tt_lru)
            h = _matmul_residual(act, lru_w_out[j].astype(bf16), h, gate_m, S, tm, 512)
        else:
            w_in = gdn_w_in[j]
            n_main = GDN_CONV_DIM + GDN_V_DIM
            pm, ba = _norm_matmul(h, nw, scale_m, shift_m, w_in[:, :n_main].astype(bf16),
                                  w_in[:, n_main:].astype(bf16), gdn_conv_w[j],
                                  jnp.zeros((1, GDN_CONV_DIM), f32), True, S, tm, 1024)
            beta, gc = _gdn_gates(ba, gdn_a_log[j].reshape(1, -1), gdn_dt_bias[j].reshape(1, -1),
                                  min(512, S))
            act = _gdn_core(pm, gc, beta, gdn_norm[j].reshape(1, -1), B, S, tt_gdn, 4)
            h = _matmul_residual(act, gdn_w_out[j].astype(bf16), h, gate_m, S, tm, 512)
        h = _moe_layer(h, norm_ffn[layer].reshape(1, D), scale_f, shift_f, gate_f,
                       moe_w_router[layer], moe_b_router[layer], moe_w_gate_up[layer],
                       moe_b_gate_up[layer], moe_w_down[layer], moe_b_down[layer], S)
    return _final_norm(h, norm_final.reshape(1, D), tm).reshape(B, S, D)
```

```python
import functools

import jax
import jax.numpy as jnp
from jax import lax
from jax.experimental import pallas as pl
from jax.experimental.pallas import tpu as pltpu

f32 = jnp.float32
bf16 = jnp.bfloat16
i32 = jnp.int32

N_MODS = 6
CONV_WIDTH = 4
RMS_EPS = 1e-6
L2_EPS = 1e-6

LRU_HEADS = 16
LRU_HEAD_DIM = 128
LRU_C = 8.0

GDN_HEAD_DIM = 128
GDN_KEY_HEADS = 16
GDN_VALUE_HEADS = 32
GDN_QK_DIM = GDN_KEY_HEADS * GDN_HEAD_DIM
GDN_V_DIM = GDN_VALUE_HEADS * GDN_HEAD_DIM
GDN_CONV_DIM = 2 * GDN_QK_DIM + GDN_V_DIM
CHUNK = 64

N_EXPERTS = 32
TOP_K = 4
SWIGLU_ALPHA = 1.702
SWIGLU_LIMIT = 7.0
MOE_ROWS = 512

VMEM_LIMIT_BYTES = 56 * 1024 * 1024
SUBLANES = 8


def _params(*sem):
    return pltpu.CompilerParams(dimension_semantics=sem, vmem_limit_bytes=VMEM_LIMIT_BYTES)


def _dot(a, b):
    return jnp.dot(a.astype(bf16), b.astype(bf16), preferred_element_type=f32)


def _dot_nt(a, b):
    return lax.dot_general(a.astype(bf16), b.astype(bf16), (((1,), (1,)), ((), ())),
                           preferred_element_type=f32)


def _dot_tn(a, b):
    return lax.dot_general(a.astype(bf16), b.astype(bf16), (((0,), (0,)), ((), ())),
                           preferred_element_type=f32)


def _silu(x):
    return x * jax.nn.sigmoid(x)


def _softplus(x):
    return jnp.maximum(x, 0.0) + jnp.log(1.0 + jnp.exp(-jnp.abs(x)))


def _modulated_norm(x, nw, scale, shift):
    ms = jnp.mean(x * x, axis=-1, keepdims=True)
    y = x * lax.rsqrt(ms + RMS_EPS) * nw
    return y * (1.0 + scale) + shift


def _ada_body(c_ref, w_ref, o_ref):
    o_ref[...] = _dot(_silu(c_ref[...]), w_ref[...])


def _ada(c, w_ada):
    B, D = c.shape
    N = w_ada.shape[1]
    tn = 1024
    return pl.pallas_call(
        _ada_body,
        grid=(N // tn,),
        in_specs=[pl.BlockSpec((B, D), lambda j: (0, 0)),
                  pl.BlockSpec((D, tn), lambda j: (0, j))],
        out_specs=pl.BlockSpec((B, tn), lambda j: (0, j)),
        out_shape=jax.ShapeDtypeStruct((B, N), f32),
        compiler_params=_params("arbitrary"),
        name="ada",
    )(c, w_ada)


def _norm_mm_body(h_ref, nw_ref, sc_ref, sh_ref, w_ref, *rest, has_small):
    if has_small:
        ws_ref, o_ref, os_ref, u_ref = rest
    else:
        o_ref, u_ref = rest

    @pl.when(pl.program_id(1) == 0)
    def _():
        u = _modulated_norm(h_ref[...], nw_ref[...], sc_ref[0], sh_ref[0]).astype(bf16)
        u_ref[...] = u
        if has_small:
            os_ref[...] = jnp.dot(u, ws_ref[...], preferred_element_type=f32)

    o_ref[...] = jnp.dot(u_ref[...], w_ref[...], preferred_element_type=f32).astype(o_ref.dtype)


def _norm_matmul(h, nw, scale, shift, w, w_small, S, tm, tn):
    T, D = h.shape
    N = w.shape[1]
    tpb = S // tm
    in_specs = [pl.BlockSpec((tm, D), lambda i, j: (i, 0)),
                pl.BlockSpec((1, D), lambda i, j: (0, 0)),
                pl.BlockSpec((1, 1, D), lambda i, j: (i // tpb, 0, 0)),
                pl.BlockSpec((1, 1, D), lambda i, j: (i // tpb, 0, 0)),
                pl.BlockSpec((D, tn), lambda i, j: (0, j))]
    out_specs = [pl.BlockSpec((tm, tn), lambda i, j: (i, j))]
    out_shape = [jax.ShapeDtypeStruct((T, N), f32)]
    args = [h, nw, scale, shift, w]
    if w_small is not None:
        ns = w_small.shape[1]
        in_specs.append(pl.BlockSpec((D, ns), lambda i, j: (0, 0)))
        out_specs.append(pl.BlockSpec((tm, ns), lambda i, j: (i, 0)))
        out_shape.append(jax.ShapeDtypeStruct((T, ns), f32))
        args.append(w_small)
    res = pl.pallas_call(
        functools.partial(_norm_mm_body, has_small=w_small is not None),
        grid=(T // tm, N // tn),
        in_specs=in_specs, out_specs=out_specs, out_shape=out_shape,
        scratch_shapes=[pltpu.VMEM((tm, D), bf16)],
        compiler_params=_params("arbitrary", "arbitrary"),
        name="norm_matmul",
    )(*args)
    return res if w_small is not None else res[0]


def _mm_res_body(a_ref, w_ref, h_ref, g_ref, o_ref):
    acc = jnp.dot(a_ref[...], w_ref[...], preferred_element_type=f32)
    o_ref[...] = h_ref[...] + g_ref[0] * acc


def _matmul_residual(a, w, h, gate, S, tm, tn):
    T, K = a.shape
    D = w.shape[1]
    tpb = S // tm
    return pl.pallas_call(
        _mm_res_body,
        grid=(T // tm, D // tn),
        in_specs=[pl.BlockSpec((tm, K), lambda i, j: (i, 0)),
                  pl.BlockSpec((K, tn), lambda i, j: (0, j)),
                  pl.BlockSpec((tm, tn), lambda i, j: (i, j)),
                  pl.BlockSpec((1, 1, tn), lambda i, j: (i // tpb, 0, j))],
        out_specs=pl.BlockSpec((tm, tn), lambda i, j: (i, j)),
        out_shape=jax.ShapeDtypeStruct((T, D), f32),
        compiler_params=_params("arbitrary", "arbitrary"),
        name="matmul_residual",
    )(a, w, h, gate)


def _causal_conv(x_ref, pad_ref, cw_ref, tt):
    pad_ref[SUBLANES:SUBLANES + tt, :] = x_ref[...].astype(f32)
    base = SUBLANES - (CONV_WIDTH - 1)
    acc = cw_ref[0:1, :] * pad_ref[pl.ds(base, tt), :]
    for k in range(1, CONV_WIDTH):
        acc = acc + cw_ref[k:k + 1, :] * pad_ref[pl.ds(base + k, tt), :]
    pad_ref[0:SUBLANES, :] = pad_ref[tt:tt + SUBLANES, :]
    return acc


def _lru_body(xb_ref, yb_ref, cw_ref, cb_ref, wg_ref, brg_ref, big_ref, lam_ref, o_ref,
              pad_ref, hstate_ref, a_ref, b_ref, hs_ref, *, tt):
    @pl.when(pl.program_id(1) == 0)
    def _():
        pad_ref[0:SUBLANES, :] = jnp.zeros((SUBLANES, pad_ref.shape[1]), f32)
        hstate_ref[...] = jnp.zeros_like(hstate_ref)

    xc = _causal_conv(xb_ref, pad_ref, cw_ref, tt) + cb_ref[...]
    neg_c_sp = -LRU_C * _softplus(-lam_ref[...])
    hd = LRU_HEAD_DIM
    for hh in range(LRU_HEADS):
        sl = slice(hh * hd, (hh + 1) * hd)
        xh = xc[:, sl]
        gates = _dot(xh, wg_ref[hh])
        r = jax.nn.sigmoid(gates[:, :hd] + brg_ref[:, sl])
        ig = jax.nn.sigmoid(gates[:, hd:] + big_ref[:, sl])
        a = jnp.exp(r * neg_c_sp[:, sl])
        a_ref[:, sl] = a
        b_ref[:, sl] = jnp.sqrt(jnp.maximum(1.0 - a * a, 0.0)) * (ig * xh)

    def step(i, h):
        h = a_ref[pl.ds(i, 1), :] * h + b_ref[pl.ds(i, 1), :]
        hs_ref[pl.ds(i, 1), :] = h
        return h

    hstate_ref[...] = lax.fori_loop(0, tt, step, hstate_ref[...], unroll=8)
    o_ref[...] = (hs_ref[...] * jax.nn.gelu(yb_ref[...].astype(f32), approximate=True)).astype(o_ref.dtype)


def _lru_core(proj, conv_w, conv_b, w_gates, b_rg, b_ig, lam, B, S, tt):
    T = proj.shape[0]
    D = proj.shape[1] // 2
    nt = S // tt
    vec = pl.BlockSpec((1, D), lambda b, t: (0, 0))
    return pl.pallas_call(
        functools.partial(_lru_body, tt=tt),
        grid=(B, nt),
        in_specs=[pl.BlockSpec((tt, D), lambda b, t: (b * nt + t, 0)),
                  pl.BlockSpec((tt, D), lambda b, t: (b * nt + t, 1)),
                  pl.BlockSpec((CONV_WIDTH, D), lambda b, t: (0, 0)),
                  vec,
                  pl.BlockSpec((LRU_HEADS, LRU_HEAD_DIM, 2 * LRU_HEAD_DIM), lambda b, t: (0, 0, 0)),
                  vec, vec, vec],
        out_specs=pl.BlockSpec((tt, D), lambda b, t: (b * nt + t, 0)),
        out_shape=jax.ShapeDtypeStruct((T, D), bf16),
        scratch_shapes=[pltpu.VMEM((tt + SUBLANES, D), f32),
                        pltpu.VMEM((1, D), f32),
                        pltpu.VMEM((tt, D), f32),
                        pltpu.VMEM((tt, D), f32),
                        pltpu.VMEM((tt, D), f32)],
        compiler_params=_params("arbitrary", "arbitrary"),
        name="lru_core",
    )(proj, proj, conv_w, conv_b, w_gates, b_rg, b_ig, lam)


def _gdn_gate_body(ba_ref, alog_ref, dtb_ref, beta_ref, gc_ref, *, tt):
    nh = GDN_VALUE_HEADS
    ba = ba_ref[...]
    beta_ref[...] = jax.nn.sigmoid(ba[:, :nh])
    g = -jnp.exp(alog_ref[...]) * _softplus(ba[:, nh:] + dtb_ref[...])
    ri = lax.broadcasted_iota(i32, (tt, tt), 0)
    ci = lax.broadcasted_iota(i32, (tt, tt), 1)
    same_chunk = (ri // CHUNK) == (ci // CHUNK)
    tri = jnp.where(same_chunk & (ci <= ri), 1.0, 0.0).astype(f32)
    gc_ref[...] = jnp.dot(tri, g, precision=lax.Precision.HIGHEST, preferred_element_type=f32)


def _gdn_gates(ba, a_log, dt_bias, tt):
    T = ba.shape[0]
    nh = GDN_VALUE_HEADS
    out = jax.ShapeDtypeStruct((T, nh), f32)
    return pl.pallas_call(
        functools.partial(_gdn_gate_body, tt=tt),
        grid=(T // tt,),
        in_specs=[pl.BlockSpec((tt, 2 * nh), lambda i: (i, 0)),
                  pl.BlockSpec((1, nh), lambda i: (0, 0)),
                  pl.BlockSpec((1, nh), lambda i: (0, 0))],
        out_specs=[pl.BlockSpec((tt, nh), lambda i: (i, 0)),
                   pl.BlockSpec((tt, nh), lambda i: (i, 0))],
        out_shape=[out, out],
        compiler_params=_params("arbitrary"),
        name="gdn_gates",
    )(ba, a_log, dt_bias)


def _gdn_body(q_ref, k_ref, v_ref, z_ref, cwq_ref, cwk_ref, cwv_ref, gc_ref, be_ref, nw_ref, o_ref,
              qpad_ref, kpad_ref, vpad_ref, state_ref, *, G, tt):
    hd = GDN_HEAD_DIM
    group = GDN_VALUE_HEADS // GDN_KEY_HEADS

    @pl.when(pl.program_id(2) == 0)
    def _():
        for p in (qpad_ref, kpad_ref, vpad_ref):
            p[0:SUBLANES, :] = jnp.zeros((SUBLANES, p.shape[1]), f32)
        state_ref[...] = jnp.zeros_like(state_ref)

    q = _silu(_causal_conv(q_ref, qpad_ref, cwq_ref, tt))
    k = _silu(_causal_conv(k_ref, kpad_ref, cwk_ref, tt))
    v = _silu(_causal_conv(v_ref, vpad_ref, cwv_ref, tt))
    gcb = gc_ref[...]
    beb = be_ref[...]
    nchunk = tt // CHUNK
    heads = G * group
    pw_rows = group * CHUNK
    assert pw_rows == hd
    ri = lax.broadcasted_iota(i32, (pw_rows, pw_rows), 0)
    ci = lax.broadcasted_iota(i32, (pw_rows, pw_rows), 1)
    same_head = (ri // CHUNK) == (ci // CHUNK)
    causal = same_head & (ri >= ci)
    strict = same_head & (ri > ci)
    eye = ri == ci
    col_head = lax.broadcasted_iota(i32, (1, pw_rows), 1) // CHUNK

    ones_hd = jnp.ones((hd, hd), bf16)

    def row_sumsq(x):
        return jnp.dot((x * x).astype(bf16), ones_hd, preferred_element_type=f32)

    qn, kn = [], []
    for kh in range(G):
        qh = q[:, kh * hd:(kh + 1) * hd]
        kk = k[:, kh * hd:(kh + 1) * hd]
        qn.append(qh * (lax.rsqrt(row_sumsq(qh) + L2_EPS) * (hd ** -0.5)))
        kn.append(kk * lax.rsqrt(row_sumsq(kk) + L2_EPS))
    pick_shape = (GDN_VALUE_HEADS, heads * hd)
    picked_head = pl.program_id(1) * heads + lax.broadcasted_iota(i32, pick_shape, 1) // hd
    pick = jnp.where(lax.broadcasted_iota(i32, pick_shape, 0) == picked_head, 1.0, 0.0).astype(bf16)
    g_hi = gcb.astype(bf16)
    g_lo = (gcb - g_hi.astype(f32)).astype(bf16)
    g_rep = (jnp.dot(g_hi, pick, preferred_element_type=f32)
             + jnp.dot(g_lo, pick, preferred_element_type=f32))
    b_rep = jnp.dot(beb.astype(bf16), pick, preferred_element_type=f32)
    g_all = [g_rep[:, hl * hd:(hl + 1) * hd] for hl in range(heads)]
    b_all = [b_rep[:, hl * hd:(hl + 1) * hd] for hl in range(heads)]

    def stack_heads(per_head):
        return jnp.concatenate(per_head, axis=0)

    items = [(kh, c) for c in range(nchunk) for kh in range(G)]
    rows_of = lambda c: slice(c * CHUNK, (c + 1) * CHUNK)
    decay, a_mat, xs, q_dec, k_dec_t, s_gain, qk = {}, {}, {}, {}, {}, {}, {}
    for p in items:
        kh, c = p
        hs = [kh * group + j for j in range(group)]
        kc = kn[kh][rows_of(c)]
        k2 = stack_heads([kc] * group)
        q2 = stack_heads([qn[kh][rows_of(c)]] * group)
        kt = kc.T
        kt2 = jnp.concatenate([kt] * group, axis=1)
        gcol = stack_heads([g_all[h][rows_of(c)] for h in hs])
        bcol = stack_heads([b_all[h][rows_of(c)] for h in hs])
        grow = jnp.sum(jnp.where(eye, gcol, 0.0), axis=0, keepdims=True)
        decay[p] = jnp.where(causal, jnp.exp(gcol - grow), 0.0)
        a_mat[p] = jnp.where(strict, bcol * _dot(k2, kt2) * decay[p], 0.0)
        qk[p] = _dot(q2, kt2) * decay[p]
        eg = jnp.exp(gcol)
        q_dec[p] = q2 * eg
        v2 = stack_heads([v[:, h * hd:(h + 1) * hd][rows_of(c)] for h in hs])
        xs[p] = jnp.concatenate([v2 * bcol, k2 * (bcol * eg)], axis=1)
        lasts = [g_all[h][c * CHUNK + CHUNK - 1:(c + 1) * CHUNK, :] for h in hs]
        gl_row = lasts[0]
        for j in range(1, group):
            gl_row = jnp.where(col_head == j, lasts[j], gl_row)
        for j in range(group):
            s_gain[kh * group + j, c] = jnp.exp(lasts[j])
        kd = kt2 * jnp.exp(gl_row - grow)
        k_dec_t[p] = stack_heads([jnp.where(col_head == j, kd, 0.0) for j in range(group)])
    for p in items:
        xs[p] = xs[p] - _dot(a_mat[p], xs[p])
    pw = dict(a_mat)
    span = 2
    while span < CHUNK:
        for p in items:
            pw[p] = _dot(pw[p], pw[p])
        for p in items:
            xs[p] = xs[p] + _dot(pw[p], xs[p])
        span *= 2
    lhs, o_add, s_add = {}, {}, {}
    for p in items:
        kh, c = p
        pq = _dot(qk[p], xs[p])
        mk = _dot(k_dec_t[p], xs[p])
        q_eff = q_dec[p] - pq[:, hd:]
        for j in range(group):
            n = (kh * group + j, c)
            o_add[n] = pq[j * CHUNK:(j + 1) * CHUNK, :hd]
            s_add[n] = mk[j * hd:(j + 1) * hd, :hd]
            lhs[n] = jnp.concatenate([q_eff[j * CHUNK:(j + 1) * CHUNK], mk[j * hd:(j + 1) * hd, hd:]], axis=0)
    st = [state_ref[hl] for hl in range(heads)]
    outs = [[] for _ in range(heads)]
    for c in range(nchunk):
        for hl in range(heads):
            n = (hl, c)
            r = _dot(lhs[n], st[hl])
            outs[hl].append(r[:CHUNK] + o_add[n])
            st[hl] = st[hl] * s_gain[n] - r[CHUNK:] + s_add[n]
    for hl in range(heads):
        state_ref[hl] = st[hl]
        o_all = jnp.concatenate(outs[hl], axis=0)
        zz = z_ref[:, hl * hd:(hl + 1) * hd].astype(f32)
        on = o_all * lax.rsqrt(row_sumsq(o_all) * (1.0 / hd) + RMS_EPS) * nw_ref[...]
        o_ref[:, hl * hd:(hl + 1) * hd] = (on * _silu(zz)).astype(o_ref.dtype)


def _gdn_core(pm, conv_w, gc, beta, norm_w, B, S, tt, G):
    T = pm.shape[0]
    hd = GDN_HEAD_DIM
    nt = S // tt
    qw = G * hd
    vw = 2 * G * hd
    nq = GDN_QK_DIM // qw
    nv = GDN_V_DIM // vw
    row = lambda b, g, t: b * nt + t
    return pl.pallas_call(
        functools.partial(_gdn_body, G=G, tt=tt),
        grid=(B, GDN_KEY_HEADS // G, nt),
        in_specs=[pl.BlockSpec((tt, qw), lambda b, g, t: (row(b, g, t), g)),
                  pl.BlockSpec((tt, qw), lambda b, g, t: (row(b, g, t), nq + g)),
                  pl.BlockSpec((tt, vw), lambda b, g, t: (row(b, g, t), nv + g)),
                  pl.BlockSpec((tt, vw), lambda b, g, t: (row(b, g, t), 2 * nv + g)),
                  pl.BlockSpec((CONV_WIDTH, qw), lambda b, g, t: (0, g)),
                  pl.BlockSpec((CONV_WIDTH, qw), lambda b, g, t: (0, nq + g)),
                  pl.BlockSpec((CONV_WIDTH, vw), lambda b, g, t: (0, nv + g)),
                  pl.BlockSpec((tt, GDN_VALUE_HEADS), lambda b, g, t: (row(b, g, t), 0)),
                  pl.BlockSpec((tt, GDN_VALUE_HEADS), lambda b, g, t: (row(b, g, t), 0)),
                  pl.BlockSpec((1, hd), lambda b, g, t: (0, 0))],
        out_specs=pl.BlockSpec((tt, vw), lambda b, g, t: (row(b, g, t), g)),
        out_shape=jax.ShapeDtypeStruct((T, GDN_V_DIM), bf16),
        scratch_shapes=[pltpu.VMEM((tt + SUBLANES, qw), f32),
                        pltpu.VMEM((tt + SUBLANES, qw), f32),
                        pltpu.VMEM((tt + SUBLANES, vw), f32),
                        pltpu.VMEM((2 * G, hd, hd), f32)],
        compiler_params=_params("arbitrary", "arbitrary", "arbitrary"),
        name="gdn_core",
    )(pm, pm, pm, pm, conv_w, conv_w, conv_w, gc, beta, norm_w)


def _router_body(h_ref, nw_ref, sc_ref, sh_ref, wr_ref, br_ref,
                 u_ref, ids_ref, wts_ref, rk_ref, cnt_ref, *, tm):
    @pl.when(pl.program_id(0) == 0)
    def _():
        cnt_ref[...] = jnp.zeros_like(cnt_ref)

    u = _modulated_norm(h_ref[...], nw_ref[...], sc_ref[0], sh_ref[0])
    u_ref[...] = u
    logits = lax.dot_general(wr_ref[...], u, (((1,), (1,)), ((), ())),
                             precision=lax.Precision.HIGHEST,
                             preferred_element_type=f32) + br_ref[...]
    eidx = lax.broadcasted_iota(i32, logits.shape, 0)
    cur = logits
    vals, sels = [], []
    for _ in range(TOP_K):
        m = jnp.max(cur, axis=0, keepdims=True)
        sel = jnp.min(jnp.where(cur == m, eidx, N_EXPERTS), axis=0, keepdims=True)
        vals.append(m)
        sels.append(sel)
        cur = jnp.where(eidx == sel, -jnp.inf, cur)
    ex = [jnp.exp(v - vals[0]) for v in vals]
    den = ex[0] + ex[1] + ex[2] + ex[3]
    wts_ref[...] = jnp.concatenate([e / den for e in ex], axis=0)
    ids_ref[...] = jnp.concatenate(sels, axis=0)
    onehots = [eidx == s for s in sels]
    chosen = jnp.zeros(logits.shape, f32)
    for oh in onehots:
        chosen = chosen + jnp.where(oh, 1.0, 0.0)
    ti = lax.broadcasted_iota(i32, (tm, tm), 0)
    tj = lax.broadcasted_iota(i32, (tm, tm), 1)
    before = jnp.where(ti < tj, 1.0, 0.0)
    base = _dot(chosen, before) + cnt_ref[:, 0:1]
    ranks = [jnp.sum(jnp.where(oh, base, 0.0), axis=0, keepdims=True) for oh in onehots]
    rk_ref[...] = jnp.concatenate(ranks, axis=0).astype(i32)
    cnt_ref[...] = cnt_ref[...] + jnp.sum(chosen, axis=1, keepdims=True)


def _router(h, nw, scale, shift, w_r_t, b_r, S, tm):
    T, D = h.shape
    tpb = S // tm
    lanes = 128
    return pl.pallas_call(
        functools.partial(_router_body, tm=tm),
        grid=(T // tm,),
        in_specs=[pl.BlockSpec((tm, D), lambda i: (i, 0)),
                  pl.BlockSpec((1, D), lambda i: (0, 0)),
                  pl.BlockSpec((1, 1, D), lambda i: (i // tpb, 0, 0)),
                  pl.BlockSpec((1, 1, D), lambda i: (i // tpb, 0, 0)),
                  pl.BlockSpec((N_EXPERTS, D), lambda i: (0, 0)),
                  pl.BlockSpec((N_EXPERTS, 1), lambda i: (0, 0))],
        out_specs=[pl.BlockSpec((tm, D), lambda i: (i, 0)),
                   pl.BlockSpec((TOP_K, tm), lambda i: (0, i)),
                   pl.BlockSpec((TOP_K, tm), lambda i: (0, i)),
                   pl.BlockSpec((TOP_K, tm), lambda i: (0, i)),
                   pl.BlockSpec((N_EXPERTS, lanes), lambda i: (0, 0))],
        out_shape=[jax.ShapeDtypeStruct((T, D), f32),
                   jax.ShapeDtypeStruct((TOP_K, T), i32),
                   jax.ShapeDtypeStruct((TOP_K, T), f32),
                   jax.ShapeDtypeStruct((TOP_K, T), i32),
                   jax.ShapeDtypeStruct((N_EXPERTS, lanes), f32)],
        compiler_params=_params("arbitrary"),
        name="moe_router",
    )(h, nw, scale, shift, w_r_t, b_r)


ISSUE_GROUP = 16


class _RowGather:
    def __init__(self, idx_hbm, src_hbm, idx_smem, buf, isem, gsem, n_idx):
        self.idx_hbm, self.src_hbm, self.idx_smem, self.buf = idx_hbm, src_hbm, idx_smem, buf
        self.isem, self.gsem, self.n_idx = isem, gsem, n_idx

    def idx_copy(self, blk, s):
        return pltpu.make_async_copy(self.idx_hbm.at[blk], self.idx_smem.at[s], self.isem.at[s])

    def row_copy(self, s, r, row):
        return pltpu.make_async_copy(self.src_hbm.at[pl.ds(row, 1), :],
                                     self.buf.at[s, pl.ds(r, 1), :], self.gsem.at[s])

    def issue_range(self, s, r0, n):
        rows = [self.idx_smem[s, 0, r0 + j] for j in range(n)]
        for j in range(n):
            self.row_copy(s, r0 + j, rows[j]).start()

    def issue_rows(self, s):
        def body(g, carry):
            self.issue_range(s, pl.multiple_of(g * ISSUE_GROUP, ISSUE_GROUP), ISSUE_GROUP)
            return carry
        lax.fori_loop(0, self.n_idx // ISSUE_GROUP, body, 0)

    def issue_rows_unrolled(self, s):
        for g in range(self.n_idx // ISSUE_GROUP):
            self.issue_range(s, g * ISSUE_GROUP, ISSUE_GROUP)

    def wait_rows(self, s):
        def body(r, carry):
            self.row_copy(s, r, 0).wait()
            return carry
        lax.fori_loop(0, self.n_idx, body, 0, unroll=8)


def _gather_pipeline(gather):
    i = pl.program_id(0)
    nblk = pl.num_programs(0)
    slot = lax.rem(i, 2)
    nslot = 1 - slot

    @pl.when(i == 0)
    def _():
        first = gather.idx_copy(0, 0)
        first.start()
        first.wait()
        gather.issue_rows(0)

        @pl.when(nblk > 1)
        def _():
            gather.idx_copy(1, 1).start()

    for par in (0, 1):
        @pl.when(jnp.logical_and(i + 1 < nblk, slot == par))
        def _(par=par):
            gather.idx_copy(i + 1, 1 - par).wait()
            gather.issue_rows_unrolled(1 - par)

    @pl.when(i + 2 < nblk)
    def _():
        gather.idx_copy(i + 2, slot).start()

    gather.wait_rows(slot)
    return slot


MOE_NTILE = 256


def _moe_body(be_ref, nu_ref, rt_hbm, u_hbm, wgu_ref, bgu_ref, wd_ref, bd_ref, y_ref,
              idx_smem, xbuf, isem, gsem):
    gather = _RowGather(rt_hbm, u_hbm, idx_smem, xbuf, isem, gsem, MOE_ROWS)
    i = pl.program_id(0)
    n_used = nu_ref[0]
    used = i < n_used
    slot = lax.rem(i, 2)
    nslot = 1 - slot

    @pl.when(i == 0)
    def _():
        first = gather.idx_copy(0, 0)
        first.start()
        first.wait()
        gather.issue_rows(0)
        gather.idx_copy(1, 1).start()

    @pl.when(i <= n_used)
    def _():
        gather.wait_rows(slot)

    def multiply(slot, nslot):
        gather.idx_copy(i + 1, nslot).wait()
        dff = wd_ref.shape[1]
        d_out = wd_ref.shape[2]
        n_up = 2 * dff // MOE_NTILE
        n_down = d_out // MOE_NTILE
        n_chunks = n_up + n_down
        bounds = [(MOE_ROWS * c) // n_chunks for c in range(n_chunks + 1)]

        def issue_chunk(c):
            gather.issue_range(nslot, bounds[c], bounds[c + 1] - bounds[c])

        x = xbuf[slot].astype(bf16)
        gu = []
        for j in range(n_up):
            issue_chunk(j)
            cols = slice(j * MOE_NTILE, (j + 1) * MOE_NTILE)
            gu.append(jnp.dot(x, wgu_ref[0, :, cols], preferred_element_type=f32) + bgu_ref[0, :, cols])
        gu = jnp.concatenate(gu, axis=1)
        glu = jnp.minimum(gu[:, :dff], SWIGLU_LIMIT)
        lin = jnp.clip(gu[:, dff:], -SWIGLU_LIMIT, SWIGLU_LIMIT)
        hid = (glu * jax.nn.sigmoid(SWIGLU_ALPHA * glu) * (lin + 1.0)).astype(bf16)
        for j in range(n_down):
            issue_chunk(n_up + j)
            cols = slice(j * MOE_NTILE, (j + 1) * MOE_NTILE)
            y_ref[:, cols] = jnp.dot(hid, wd_ref[0, :, cols], preferred_element_type=f32) + bd_ref[0, :, cols]

    for par in (0, 1):
        @pl.when(jnp.logical_and(used, slot == par))
        def _(par=par):
            multiply(par, 1 - par)

    @pl.when(i + 1 < n_used)
    def _():
        gather.idx_copy(i + 2, slot).start()

    @pl.when(jnp.logical_not(used))
    def _():
        y_ref[...] = jnp.zeros_like(y_ref)


def _moe_experts(blk_expert, n_used, row_tok, u, w_gu, b_gu, w_down, b_down):
    nblk = row_tok.shape[0]
    assert nblk >= 2
    T, D = u.shape
    dff2 = w_gu.shape[2]
    dff = w_down.shape[1]
    grid_spec = pltpu.PrefetchScalarGridSpec(
        num_scalar_prefetch=2,
        grid=(nblk,),
        in_specs=[pl.BlockSpec(memory_space=pl.ANY),
                  pl.BlockSpec(memory_space=pl.ANY),
                  pl.BlockSpec((1, D, dff2), lambda i, be, nu: (be[i], 0, 0)),
                  pl.BlockSpec((1, 1, dff2), lambda i, be, nu: (be[i], 0, 0)),
                  pl.BlockSpec((1, dff, D), lambda i, be, nu: (be[i], 0, 0)),
                  pl.BlockSpec((1, 1, D), lambda i, be, nu: (be[i], 0, 0))],
        out_specs=pl.BlockSpec((MOE_ROWS, D), lambda i, be, nu: (i, 0)),
        scratch_shapes=[pltpu.SMEM((2, 1, MOE_ROWS), i32),
                        pltpu.VMEM((2, MOE_ROWS, D), f32),
                        pltpu.SemaphoreType.DMA((2,)),
                        pltpu.SemaphoreType.DMA((2,))],
    )
    return pl.pallas_call(
        _moe_body,
        grid_spec=grid_spec,
        out_shape=jax.ShapeDtypeStruct((nblk * MOE_ROWS, D), f32),
        compiler_params=_params("arbitrary"),
        name="moe_experts",
    )(blk_expert, n_used, row_tok, u, w_gu, b_gu, w_down, b_down)


def _combine_body(d_hbm, y_hbm, h_ref, w_ref, g_ref, o_ref, idx_smem, ybuf, isem, gsem, *, tm):
    slot = _gather_pipeline(_RowGather(d_hbm, y_hbm, idx_smem, ybuf, isem, gsem, TOP_K * tm))
    w = w_ref[...]
    mix = w[:, 0:1] * ybuf[slot, 0:tm, :]
    for k in range(1, TOP_K):
        mix = mix + w[:, k:k + 1] * ybuf[slot, k * tm:(k + 1) * tm, :]
    o_ref[...] = h_ref[...] + g_ref[0] * mix


def _moe_combine(dest_blk, y, h, wts_t, gate, S, tm):
    T, D = h.shape
    tpb = S // tm
    return pl.pallas_call(
        functools.partial(_combine_body, tm=tm),
        grid=(T // tm,),
        in_specs=[pl.BlockSpec(memory_space=pl.ANY),
                  pl.BlockSpec(memory_space=pl.ANY),
                  pl.BlockSpec((tm, D), lambda i: (i, 0)),
                  pl.BlockSpec((tm, TOP_K), lambda i: (i, 0)),
                  pl.BlockSpec((1, 1, D), lambda i: (i // tpb, 0, 0))],
        out_specs=pl.BlockSpec((tm, D), lambda i: (i, 0)),
        out_shape=jax.ShapeDtypeStruct((T, D), f32),
        scratch_shapes=[pltpu.SMEM((2, 1, TOP_K * tm), i32),
                        pltpu.VMEM((2, TOP_K * tm, D), f32),
                        pltpu.SemaphoreType.DMA((2,)),
                        pltpu.SemaphoreType.DMA((2,))],
        compiler_params=_params("arbitrary"),
        name="moe_combine",
    )(dest_blk, y, h, wts_t, gate)


def _moe_layer(h, nw, scale, shift, gate, w_router, b_router, w_gu, b_gu, w_down, b_down, S):
    T, D = h.shape
    tm_r = min(512, S)
    u, ids, wts, ranks, cnt = _router(h, nw, scale, shift, w_router.T, b_router.reshape(N_EXPERTS, 1),
                                      S, tm_r)
    counts = cnt[:, 0].astype(i32)
    padded = (counts + MOE_ROWS - 1) // MOE_ROWS * MOE_ROWS
    pad_end = jnp.cumsum(padded)
    pad_start = pad_end - padded
    eids = jnp.arange(N_EXPERTS, dtype=i32)[:, None, None]
    dest = jnp.sum(jnp.where(ids[None] == eids, pad_start[:, None, None], 0), axis=0) + ranks
    nblk = -(-(T * TOP_K + N_EXPERTS * (MOE_ROWS - 1)) // MOE_ROWS) + 1
    tok = jnp.broadcast_to(jnp.arange(T, dtype=i32)[None, :], (TOP_K, T))
    row_tok = jnp.zeros((nblk * MOE_ROWS,), i32).at[dest.reshape(-1)].set(
        tok.reshape(-1), unique_indices=True)
    blk_start = jnp.arange(nblk, dtype=i32) * MOE_ROWS
    blk_expert = jnp.minimum(jnp.sum((blk_start[:, None] >= pad_end[None, :]).astype(i32), axis=1),
                             N_EXPERTS - 1)
    n_used = (pad_end[-1:] // MOE_ROWS).astype(i32)
    y = _moe_experts(blk_expert, n_used, row_tok.reshape(nblk, 1, MOE_ROWS), u,
                     w_gu.astype(bf16), b_gu.reshape(N_EXPERTS, 1, -1),
                     w_down.astype(bf16), b_down.reshape(N_EXPERTS, 1, -1))
    tm_c = min(256, S)
    dest_blk = dest.reshape(TOP_K, T // tm_c, tm_c).transpose(1, 0, 2).reshape(T // tm_c, 1, TOP_K * tm_c)
    return _moe_combine(dest_blk, y, h, wts.T, gate, S, tm_c)


def _final_norm_body(h_ref, nw_ref, o_ref):
    x = h_ref[...]
    ms = jnp.mean(x * x, axis=-1, keepdims=True)
    o_ref[...] = x * lax.rsqrt(ms + RMS_EPS) * nw_ref[...]


def _final_norm(h, nw, tm):
    T, D = h.shape
    return pl.pallas_call(
        _final_norm_body,
        grid=(T // tm,),
        in_specs=[pl.BlockSpec((tm, D), lambda i: (i, 0)),
                  pl.BlockSpec((1, D), lambda i: (0, 0))],
        out_specs=pl.BlockSpec((tm, D), lambda i: (i, 0)),
        out_shape=jax.ShapeDtypeStruct((T, D), f32),
        compiler_params=_params("arbitrary"),
        name="final_norm",
    )(h, nw)


def kernel(x, c, w_ada, ada_table, norm_mix, norm_ffn, norm_final, lru_w_in, lru_conv_w, lru_conv_b, lru_w_rg, lru_b_rg, lru_w_ig, lru_b_ig, lru_lambda, lru_w_out, gdn_w_in, gdn_conv_w, gdn_a_log, gdn_dt_bias, gdn_norm, gdn_w_out, moe_w_router, moe_b_router, moe_w_gate_up, moe_b_gate_up, moe_w_down, moe_b_down):
    B, S, D = x.shape
    T = B * S
    depth = ada_table.shape[0]
    tm = min(1024, S)
    tt_lru = min(256, S)
    tt_gdn = min(256, S)

    cond = _ada(c, w_ada).reshape(B, N_MODS, D)
    h = x.reshape(T, D)
    for layer in range(depth):
        mods = cond + ada_table[layer]
        shift_m, scale_m, gate_m, shift_f, scale_f, gate_f = [mods[:, j:j + 1, :] for j in range(N_MODS)]
        nw = norm_mix[layer].reshape(1, D)
        j = layer // 2
        if layer % 2 == 0:
            proj = _norm_matmul(h, nw, scale_m, shift_m, lru_w_in[j].astype(bf16), None, S, tm, 1024)
            w_gates = jnp.concatenate([lru_w_rg[j], lru_w_ig[j]], axis=-1).astype(bf16)
            act = _lru_core(proj, lru_conv_w[j], lru_conv_b[j].reshape(1, D), w_gates,
                            lru_b_rg[j].reshape(1, D), lru_b_ig[j].reshape(1, D),
                            lru_lambda[j].reshape(1, D), B, S, tt_lru)
            h = _matmul_residual(act, lru_w_out[j].astype(bf16), h, gate_m, S, tm, 512)
        else:
            w_in = gdn_w_in[j]
            n_main = GDN_CONV_DIM + GDN_V_DIM
            pm, ba = _norm_matmul(h, nw, scale_m, shift_m, w_in[:, :n_main].astype(bf16),
                                  w_in[:, n_main:].astype(bf16), S, tm, 1024)
            beta, gc = _gdn_gates(ba, gdn_a_log[j].reshape(1, -1), gdn_dt_bias[j].reshape(1, -1),
                                  min(512, S))
            act = _gdn_core(pm, gdn_conv_w[j], gc, beta, gdn_norm[j].reshape(1, -1), B, S, tt_gdn, 4)
            h = _matmul_residual(act, gdn_w_out[j].astype(bf16), h, gate_m, S, tm, 512)
        h = _moe_layer(h, norm_ffn[layer].reshape(1, D), scale_f, shift_f, gate_f,
                       moe_w_router[layer], moe_b_router[layer], moe_w_gate_up[layer],
                       moe_b_gate_up[layer], moe_w_down[layer], moe_b_down[layer], S)
    return _final_norm(h, norm_final.reshape(1, D), tm).reshape(B, S, D)
```
